```python
import math
import jax, jax.numpy as jnp
from jax import lax
import numpy as np

D_MODEL = 1024
BATCH = 4
SEQ = 8192
DEPTH = 1

MEM_LEN = 256
HEAD_DIM = 64
MIX_W = D_MODEL
ATTN_W = MIX_W // 2
N_Q_HEADS = ATTN_W // HEAD_DIM
N_KV_HEADS = N_Q_HEADS // 4
GQA_GROUP = N_Q_HEADS // N_KV_HEADS
KV_W = N_KV_HEADS * HEAD_DIM
GM_W = MIX_W - ATTN_W
GM_HEADS = GM_W // HEAD_DIM
GM_DH = GM_W // GM_HEADS
IN_COLS = ATTN_W + 2 * KV_W + 2 * GM_W
WINDOW = 128
BLK = 128
CHUNK = 128
ROPE_THETA = 10000.0
XA_HEADS = 4
XA_DH = D_MODEL // XA_HEADS
D_FF = ((8 * D_MODEL // 3 + 127) // 128) * 128
CONV_W = 3
MAX_POS_OFFSET = 1024
EPS = 1e-6

kernel_name = "hybrid_swa_gmlp_xattn_convffn"


def rms_norm(x, g):
    xf = x.astype(jnp.float32)
    y = xf * lax.rsqrt(jnp.mean(xf * xf, axis=-1, keepdims=True) + EPS)
    return (y * g.astype(jnp.float32)).astype(x.dtype)


def rope(x, positions):
    dh = x.shape[-1]
    half = dh // 2
    inv_freq = 1.0 / (ROPE_THETA ** (jnp.arange(half, dtype=jnp.float32) * (2.0 / dh)))
    ang = positions.astype(jnp.float32)[..., None] * inv_freq
    cos = jnp.cos(ang)[:, :, None, :]
    sin = jnp.sin(ang)[:, :, None, :]
    xf = x.astype(jnp.float32)
    x1, x2 = xf[..., :half], xf[..., half:]
    out = jnp.concatenate([x1 * cos - x2 * sin, x2 * cos + x1 * sin], axis=-1)
    return out.astype(x.dtype)


def sliding_window_attn(q, k, v, sinks):
    B, S = q.shape[0], q.shape[1]
    nb = S // BLK
    qb = q.reshape(B, nb, BLK, N_KV_HEADS, GQA_GROUP, HEAD_DIM)
    kb = k.reshape(B, nb, BLK, N_KV_HEADS, HEAD_DIM)
    vb = v.reshape(B, nb, BLK, N_KV_HEADS, HEAD_DIM)
    pad = ((0, 0), (1, 0), (0, 0), (0, 0), (0, 0))
    kk = jnp.concatenate([jnp.pad(kb[:, :-1], pad), kb], axis=2)
    vv = jnp.concatenate([jnp.pad(vb[:, :-1], pad), vb], axis=2)
    scores = jnp.einsum('bnqhgd,bnkhd->bnhgqk', qb, kk).astype(jnp.float32)
    scores = scores * (1.0 / math.sqrt(HEAD_DIM))
    qi = jnp.arange(BLK)[:, None]
    kj = jnp.arange(2 * BLK)[None, :]
    diff = qi + BLK - kj
    band = (diff >= 0) & (diff < WINDOW)
    valid = (jnp.arange(nb)[:, None, None] > 0) | (kj >= BLK)[None]
    mask = (band[None] & valid)[None, :, None, None]
    scores = jnp.where(mask, scores, jnp.finfo(jnp.float32).min)
    sink = sinks.astype(jnp.float32).reshape(N_KV_HEADS, GQA_GROUP)[None, None, :, :, None, None]
    sink = jnp.broadcast_to(sink, scores.shape[:-1] + (1,))
    probs = jax.nn.softmax(jnp.concatenate([scores, sink], axis=-1), axis=-1)[..., :-1]
    out = jnp.einsum('bnhgqk,bnkhd->bnqhgd', probs.astype(v.dtype), vv)
    return out.reshape(B, S, ATTN_W)


def chunked_spatial_gating(u, v, ws, bs):
    B, S = v.shape[0], v.shape[1]
    nc = S // CHUNK
    vb = v.reshape(B, nc, CHUNK, GM_HEADS, GM_DH)
    causal = jnp.tril(jnp.ones((CHUNK, CHUNK), dtype=ws.dtype))
    mixed = jnp.einsum('hts,bnshd->bnthd', ws * causal[None], vb)
    mixed = mixed + bs.T[None, None, :, :, None]
    return u * mixed.reshape(B, S, GM_W)


def parallel_mixer(x, positions, mix_norm, w_in, q_norm, k_norm, attn_sinks,
                   gmlp_v_norm, gmlp_ws, gmlp_bs, attn_out_norm, gmlp_out_norm, w_out):
    B, S, _ = x.shape
    h = rms_norm(x, mix_norm)
    proj = h @ w_in
    q, k, v, gz = jnp.split(proj, [ATTN_W, ATTN_W + KV_W, ATTN_W + 2 * KV_W], axis=-1)
    q = rope(rms_norm(q.reshape(B, S, N_Q_HEADS, HEAD_DIM), q_norm), positions)
    k = rope(rms_norm(k.reshape(B, S, N_KV_HEADS, HEAD_DIM), k_norm), positions)
    v = v.reshape(B, S, N_KV_HEADS, HEAD_DIM)
    attn = sliding_window_attn(q, k, v, attn_sinks)
    gz = jax.nn.gelu(gz)
    gu, gv = jnp.split(gz, 2, axis=-1)
    gm = chunked_spatial_gating(gu, rms_norm(gv, gmlp_v_norm), gmlp_ws, gmlp_bs)
    y = jnp.concatenate([rms_norm(attn, attn_out_norm), rms_norm(gm, gmlp_out_norm)], axis=-1)
    return y @ w_out


def memory_cross_attn(x, mem, xa_norm, mem_norm, xa_wq, xa_wkv, xa_q_norm, xa_k_norm, xa_wo):
    B, S, _ = x.shape
    M = mem.shape[1]
    h = rms_norm(x, xa_norm)
    m = rms_norm(mem, mem_norm)
    q = rms_norm((h @ xa_wq).reshape(B, S, XA_HEADS, XA_DH), xa_q_norm)
    k, v = jnp.split(m @ xa_wkv, 2, axis=-1)
    k = rms_norm(k.reshape(B, M, XA_HEADS, XA_DH), xa_k_norm)
    v = v.reshape(B, M, XA_HEADS, XA_DH)
    scores = jnp.einsum('bshd,bmhd->bhsm', q, k).astype(jnp.float32) * (1.0 / math.sqrt(XA_DH))
    probs = jax.nn.softmax(scores, axis=-1).astype(v.dtype)
    out = jnp.einsum('bhsm,bmhd->bshd', probs, v).reshape(B, S, XA_HEADS * XA_DH)
    return out @ xa_wo


def conv_gated_ffn(x, ffn_norm, ffn_up, ffn_conv, ffn_conv_b, ffn_down):
    h = rms_norm(x, ffn_norm)
    a = h @ ffn_up
    c = lax.conv_general_dilated(
        a, ffn_conv.reshape(CONV_W, 1, 2 * D_FF).astype(a.dtype),
        window_strides=(1,), padding=[(CONV_W - 1, 0)],
        dimension_numbers=('NWC', 'WIO', 'NWC'),
        feature_group_count=2 * D_FF) + ffn_conv_b
    gate, up = jnp.split(c, 2, axis=-1)
    return (jax.nn.gelu(gate) * up) @ ffn_down


def setup_inputs(seed: int = 0) -> dict:
    key = jax.random.key(seed)
    ks = iter(jax.random.split(key, 40))
    L = DEPTH

    def nrm(shape, scale):
        return jax.random.normal(next(ks), shape, jnp.float32) * scale

    def gain(shape):
        return 1.0 + 0.02 * jax.random.normal(next(ks), shape, jnp.float32)

    x = nrm((BATCH, SEQ, D_MODEL), 1.0)
    mem = nrm((BATCH, MEM_LEN, D_MODEL), 1.0)
    offs = jax.random.randint(next(ks), (BATCH, 1), 0, MAX_POS_OFFSET, dtype=jnp.int32)
    positions = (offs + jnp.arange(SEQ, dtype=jnp.int32)[None, :]).astype(jnp.int32)
    return {
        "x": x,
        "mem": mem,
        "positions": positions,
        "mix_norm": gain((L, D_MODEL)),
        "w_in": nrm((L, D_MODEL, IN_COLS), D_MODEL ** -0.5),
        "q_norm": gain((L, HEAD_DIM)),
        "k_norm": gain((L, HEAD_DIM)),
        "attn_sinks": nrm((L, N_Q_HEADS), 0.5),
        "gmlp_v_norm": gain((L, GM_W)),
        "gmlp_ws": nrm((L, GM_HEADS, CHUNK, CHUNK), 0.5 * CHUNK ** -0.5),
        "gmlp_bs": 1.0 + nrm((L, GM_HEADS, CHUNK), 0.02),
        "attn_out_norm": gain((L, ATTN_W)),
        "gmlp_out_norm": gain((L, GM_W)),
        "w_out": nrm((L, MIX_W, D_MODEL), MIX_W ** -0.5),
        "xa_norm": gain((L, D_MODEL)),
        "mem_norm": gain((L, D_MODEL)),
        "xa_wq": nrm((L, D_MODEL, XA_HEADS * XA_DH), D_MODEL ** -0.5),
        "xa_wkv": nrm((L, D_MODEL, 2 * XA_HEADS * XA_DH), D_MODEL ** -0.5),
        "xa_q_norm": gain((L, XA_DH)),
        "xa_k_norm": gain((L, XA_DH)),
        "xa_wo": nrm((L, XA_HEADS * XA_DH, D_MODEL), (XA_HEADS * XA_DH) ** -0.5),
        "ffn_norm": gain((L, D_MODEL)),
        "ffn_up": nrm((L, D_MODEL, 2 * D_FF), D_MODEL ** -0.5),
        "ffn_conv": nrm((L, CONV_W, 2 * D_FF), CONV_W ** -0.5),
        "ffn_conv_b": nrm((L, 2 * D_FF), 0.02),
        "ffn_down": nrm((L, D_FF, D_MODEL), D_FF ** -0.5),
    }


def reference(x, mem, positions, mix_norm, w_in, q_norm, k_norm, attn_sinks,
              gmlp_v_norm, gmlp_ws, gmlp_bs, attn_out_norm, gmlp_out_norm, w_out,
              xa_norm, mem_norm, xa_wq, xa_wkv, xa_q_norm, xa_k_norm, xa_wo,
              ffn_norm, ffn_up, ffn_conv, ffn_conv_b, ffn_down):
    for l in range(DEPTH):
        x = x + parallel_mixer(x, positions, mix_norm[l], w_in[l], q_norm[l], k_norm[l],
                               attn_sinks[l], gmlp_v_norm[l], gmlp_ws[l], gmlp_bs[l],
                               attn_out_norm[l], gmlp_out_norm[l], w_out[l])
        x = x + memory_cross_attn(x, mem, xa_norm[l], mem_norm[l], xa_wq[l], xa_wkv[l],
                                  xa_q_norm[l], xa_k_norm[l], xa_wo[l])
        x = x + conv_gated_ffn(x, ffn_norm[l], ffn_up[l], ffn_conv[l], ffn_conv_b[l],
                               ffn_down[l])
    return x
```

```python
import functools
import math

import jax
import jax.numpy as jnp
from jax import lax
from jax.experimental import pallas as pl
from jax.experimental.pallas import tpu as pltpu

F32 = jnp.float32
BF16 = jnp.bfloat16

LANES = 128
SUBLANES = 8

D_MODEL = 1024
HEAD_DIM = 64
HALF = HEAD_DIM // 2
ATTN_W = 512
KV_W = 128
GM_W = 512
IN_COLS = ATTN_W + 2 * KV_W + 2 * GM_W
BLK = 128
ROPE_THETA = 10000.0
XA_HEADS = 4
XA_DH = 256
MEM_LEN = 256
D_FF = 2816
FF_CHUNK = 256
N_FF_CHUNKS = D_FF // FF_CHUNK
CONV_W = 3
EPS = 1e-6
MASK_BIAS = -1e30

TILE_M = 512
VMEM_LIMIT = 56 * 1024 * 1024


def _rms(x, g):
    return x * lax.rsqrt(jnp.mean(x * x, axis=-1, keepdims=True) + EPS) * g


def _gelu_tanh(x):
    c = math.sqrt(2.0 / math.pi)
    return x * (0.5 * (1.0 + jnp.tanh(c * (x + 0.044715 * (x * x * x)))))


def _dot(a, b):
    return jnp.dot(a, b, preferred_element_type=F32)


def _dot_nt(a, b):
    return lax.dot_general(a, b, (((1,), (1,)), ((), ())), preferred_element_type=F32)


def _const_spec(shape):
    nd = len(shape)
    return pl.BlockSpec(shape, lambda i: (0,) * nd, pipeline_mode=pl.Buffered(1))


def _rope_kernel(inv_freq_ref, pos_ref, cos_ref, sin_ref):
    pos = pos_ref[...].astype(F32)
    for f in range(HALF):
        ang = pos * inv_freq_ref[f]
        cos_ref[f] = jnp.cos(ang)
        sin_ref[f] = jnp.sin(ang)


def _rope_tables(positions, inv_freq):
    rows = positions.size // LANES
    pos2d = positions.reshape(rows, LANES)
    out = jax.ShapeDtypeStruct((HALF, rows, LANES), F32)
    return pl.pallas_call(
        _rope_kernel,
        out_shape=(out, out),
        in_specs=[pl.BlockSpec(memory_space=pltpu.SMEM),
                  pl.BlockSpec(memory_space=pltpu.VMEM)],
        out_specs=(pl.BlockSpec(memory_space=pltpu.VMEM),
                   pl.BlockSpec(memory_space=pltpu.VMEM)),
        compiler_params=pltpu.CompilerParams(vmem_limit_bytes=VMEM_LIMIT),
        name="rope_tables",
    )(inv_freq, pos2d)


def _mem_kv_kernel(mem_ref, mem_norm_ref, wkv_ref, k_norm_ref, k_ref, v_ref):
    m = _rms(mem_ref[...], mem_norm_ref[...]).astype(BF16)
    kv = _dot(m, wkv_ref[...])
    scale = 1.0 / math.sqrt(XA_DH)
    for h in range(XA_HEADS):
        cols = slice(h * XA_DH, (h + 1) * XA_DH)
        k_ref[:, cols] = (_rms(kv[:, cols], k_norm_ref[...]) * scale).astype(BF16)
    v_ref[...] = kv[:, D_MODEL:].astype(BF16)


def _mem_kv(mem2d, mem_norm, wkv, k_norm, batch):
    out = jax.ShapeDtypeStruct((batch * MEM_LEN, D_MODEL), BF16)
    blk = pl.BlockSpec((MEM_LEN, D_MODEL), lambda b: (b, 0))
    return pl.pallas_call(
        _mem_kv_kernel,
        out_shape=(out, out),
        grid=(batch,),
        in_specs=[blk, _const_spec((1, D_MODEL)), _const_spec((D_MODEL, 2 * D_MODEL)),
                  _const_spec((1, XA_DH))],
        out_specs=(blk, blk),
        compiler_params=pltpu.CompilerParams(
            dimension_semantics=("arbitrary",), vmem_limit_bytes=VMEM_LIMIT),
        name="mem_kv",
    )(mem2d, mem_norm, wkv, k_norm)


def _mixer_kernel(tiles_per_batch,
                  sinks_ref, x_ref, cos_ref, sin_ref, mix_norm_ref, w_in_ref,
                  q_norm_ref, k_norm_ref, gv_norm_ref, ws_ref, gbias_ref,
                  attn_norm_ref, gm_norm_ref, w_out_ref,
                  out_ref,
                  proj_ref, kbuf_ref, vbuf_ref, y_ref, wcat_ref, bias_ref):
    step = pl.program_id(0)
    first_tile = (step % tiles_per_batch) == 0
    n_blocks = TILE_M // BLK

    @pl.when(step == 0)
    def _build_constants():
        t = lax.broadcasted_iota(jnp.int32, (BLK, BLK), 0)
        s = lax.broadcasted_iota(jnp.int32, (BLK, BLK), 1)
        for j in range(GM_W // LANES):
            w_a = jnp.where(s <= t, ws_ref[2 * j], 0.0)
            w_b = jnp.where(s <= t, ws_ref[2 * j + 1], 0.0)
            wcat_ref[j] = jnp.concatenate([w_a, w_b], axis=1).astype(BF16)
        qi = lax.broadcasted_iota(jnp.int32, (2 * BLK, 4 * BLK), 0) & (BLK - 1)
        kj = lax.broadcasted_iota(jnp.int32, (2 * BLK, 4 * BLK), 1) & (2 * BLK - 1)
        cur = (kj >= BLK) & ((kj - BLK) <= qi)
        prev = (kj < BLK) & (kj > qi)
        bias_ref[0] = jnp.where(cur | prev, 0.0, MASK_BIAS)
        bias_ref[1] = jnp.where(cur, 0.0, MASK_BIAS)

    @pl.when(first_tile)
    def _reset_carry():
        kbuf_ref[0:BLK, :] = jnp.zeros((BLK, KV_W), F32)
        vbuf_ref[0:BLK, :] = jnp.zeros((BLK, KV_W), F32)

    h = _rms(x_ref[...], mix_norm_ref[...]).astype(BF16)
    proj_ref[...] = _dot(h, w_in_ref[...])

    lane = lax.broadcasted_iota(jnp.int32, (BLK, LANES), 1)
    low_head = lane < HEAD_DIM
    first_half = (lane & (HEAD_DIM - 1)) < HALF
    r = lax.broadcasted_iota(jnp.int32, (LANES, LANES), 0)
    c = lax.broadcasted_iota(jnp.int32, (LANES, LANES), 1)
    head_ones = jnp.where((r < HEAD_DIM) == (c < HEAD_DIM), 1.0, 0.0).astype(BF16)
    lane2 = lax.broadcasted_iota(jnp.int32, (2 * BLK, LANES), 1)
    low_head2 = lane2 < HEAD_DIM

    def head_norm_rope(t, gain, cos, sin):
        ss = _dot((t * t).astype(BF16), head_ones)
        tn = t * lax.rsqrt(ss * (1.0 / HEAD_DIM) + EPS) * gain
        rot = jnp.where(first_half, pltpu.roll(tn, LANES - HALF, 1), pltpu.roll(tn, HALF, 1))
        return tn * cos + rot * sin

    def block_body(b, carry):
        r0 = pl.multiple_of(b * BLK, BLK)
        rows = pl.ds(r0, BLK)
        cos = cos_ref[rows, :]
        sin = sin_ref[rows, :]

        k = head_norm_rope(proj_ref[rows, ATTN_W:ATTN_W + KV_W], k_norm_ref[...], cos, sin)
        kbuf_ref[pl.ds(r0 + BLK, BLK), :] = k
        vbuf_ref[pl.ds(r0 + BLK, BLK), :] = proj_ref[rows, ATTN_W + KV_W:ATTN_W + 2 * KV_W]
        kcat = kbuf_ref[pl.ds(r0, 2 * BLK), :]
        vcat = vbuf_ref[pl.ds(r0, 2 * BLK), :]
        kswp = pltpu.roll(kcat, HEAD_DIM, 1)
        vswp = pltpu.roll(vcat, HEAD_DIM, 1)

        bias = bias_ref[jnp.where(jnp.logical_and(first_tile, b == 0), 1, 0)]

        attn_parts = []
        for g in range(2):
            k_lo, k_hi = (kcat, kswp) if g == 0 else (kswp, kcat)
            v_lo, v_hi = (vcat, vswp) if g == 0 else (vswp, vcat)
            k_big = jnp.concatenate([jnp.where(low_head2, k_lo, 0.0),
                                     jnp.where(low_head2, 0.0, k_hi)], axis=0).astype(BF16)
            v_big = jnp.concatenate([jnp.where(low_head2, v_lo, 0.0),
                                     jnp.where(low_head2, 0.0, v_hi)], axis=0).astype(BF16)
            q_pairs = []
            for pair in (2 * g, 2 * g + 1):
                qcols = slice(pair * LANES, (pair + 1) * LANES)
                q = head_norm_rope(proj_ref[rows, qcols], q_norm_ref[...], cos, sin)
                q_pairs.append((q * (1.0 / math.sqrt(HEAD_DIM))).astype(BF16))
            qs = jnp.concatenate(q_pairs, axis=0)
            s = _dot_nt(qs, k_big) + bias
            p_rows, inv_rows = [], []
            for rg in range(2):
                p_cols, inv_cols = [], []
                for half in range(2):
                    sink = sinks_ref[4 * g + 2 * rg + half]
                    sh = s[rg * BLK:(rg + 1) * BLK, half * 2 * BLK:(half + 1) * 2 * BLK]
                    m = jnp.maximum(jnp.max(sh, axis=-1, keepdims=True), sink)
                    p = jnp.exp(sh - m)
                    denom = jnp.sum(p, axis=-1, keepdims=True) + jnp.exp(sink - m)
                    p_cols.append(p.astype(BF16))
                    inv_cols.append(1.0 / denom)
                p_rows.append(jnp.concatenate(p_cols, axis=1))
                inv_rows.append(jnp.where(low_head, inv_cols[0], inv_cols[1]))
            o = _dot(jnp.concatenate(p_rows, axis=0), v_big)
            attn_parts.append(o[0:BLK] * inv_rows[0])
            attn_parts.append(o[BLK:2 * BLK] * inv_rows[1])
        attn = jnp.concatenate(attn_parts, axis=1)
        y_ref[rows, 0:ATTN_W] = _rms(attn, attn_norm_ref[...]).astype(BF16)

        gz0 = ATTN_W + 2 * KV_W
        gu = _gelu_tanh(proj_ref[rows, gz0:gz0 + GM_W])
        gv = _rms(_gelu_tanh(proj_ref[rows, gz0 + GM_W:gz0 + 2 * GM_W]), gv_norm_ref[...])
        mixed = []
        for j in range(GM_W // LANES):
            vb = gv[:, j * LANES:(j + 1) * LANES]
            v_bd = jnp.concatenate([jnp.where(low_head, vb, 0.0),
                                    jnp.where(low_head, 0.0, vb)], axis=0).astype(BF16)
            mixed.append(_dot(wcat_ref[j], v_bd))
        gm = gu * (jnp.concatenate(mixed, axis=1) + gbias_ref[...])
        y_ref[rows, ATTN_W:ATTN_W + GM_W] = _rms(gm, gm_norm_ref[...]).astype(BF16)
        return carry

    lax.fori_loop(0, n_blocks, block_body, 0)

    out_ref[...] = x_ref[...] + _dot(y_ref[...], w_out_ref[...])

    kbuf_ref[0:BLK, :] = kbuf_ref[TILE_M:TILE_M + BLK, :]
    vbuf_ref[0:BLK, :] = vbuf_ref[TILE_M:TILE_M + BLK, :]


def _mixer(x2d, cos, sin, sinks, mix_norm, w_in, q_norm, k_norm, gv_norm, ws, gbias,
           attn_norm, gm_norm, w_out, tiles_per_batch):
    tokens = x2d.shape[0]
    row = lambda w: pl.BlockSpec((TILE_M, w), lambda i: (i, 0))
    return pl.pallas_call(
        functools.partial(_mixer_kernel, tiles_per_batch),
        out_shape=jax.ShapeDtypeStruct((tokens, D_MODEL), F32),
        grid=(tokens // TILE_M,),
        in_specs=[pl.BlockSpec(memory_space=pltpu.SMEM),
                  row(D_MODEL), row(LANES), row(LANES),
                  _const_spec((1, D_MODEL)), _const_spec((D_MODEL, IN_COLS)),
                  _const_spec((1, LANES)), _const_spec((1, LANES)),
                  _const_spec((1, GM_W)), _const_spec((GM_W // HEAD_DIM, BLK, BLK)),
                  _const_spec((BLK, GM_W)),
                  _const_spec((1, ATTN_W)), _const_spec((1, GM_W)),
                  _const_spec((D_MODEL, D_MODEL))],
        out_specs=row(D_MODEL),
        scratch_shapes=[
            pltpu.VMEM((TILE_M, IN_COLS), F32),
            pltpu.VMEM((TILE_M + BLK, KV_W), F32),
            pltpu.VMEM((TILE_M + BLK, KV_W), F32),
            pltpu.VMEM((TILE_M, D_MODEL), BF16),
            pltpu.VMEM((GM_W // LANES, BLK, 2 * BLK), BF16),
            pltpu.VMEM((2, 2 * BLK, 4 * BLK), F32),
        ],
        compiler_params=pltpu.CompilerParams(
            dimension_semantics=("arbitrary",), vmem_limit_bytes=VMEM_LIMIT),
        name="mixer",
    )(sinks, x2d, cos, sin, mix_norm, w_in, q_norm, k_norm, gv_norm, ws, gbias,
      attn_norm, gm_norm, w_out)


def _xattn_kernel(x_ref, xa_norm_ref, wq_ref, q_norm_ref, k_ref, v_ref, wo_ref, out_ref):
    x = x_ref[...]
    h = _rms(x, xa_norm_ref[...]).astype(BF16)
    q = _dot(h, wq_ref[...])
    outs = []
    for hd in range(XA_HEADS):
        cols = slice(hd * XA_DH, (hd + 1) * XA_DH)
        qh = _rms(q[:, cols], q_norm_ref[...]).astype(BF16)
        s = _dot_nt(qh, k_ref[:, cols])
        p = jnp.exp(s - jnp.max(s, axis=-1, keepdims=True))
        inv = 1.0 / jnp.sum(p, axis=-1, keepdims=True)
        outs.append((_dot(p.astype(BF16), v_ref[:, cols]) * inv).astype(BF16))
    out_ref[...] = x + _dot(jnp.concatenate(outs, axis=1), wo_ref[...])


def _xattn(x2d, xa_norm, wq, q_norm, k, v, wo, tiles_per_batch):
    tokens = x2d.shape[0]
    row = pl.BlockSpec((TILE_M, D_MODEL), lambda i: (i, 0))
    kv = pl.BlockSpec((MEM_LEN, D_MODEL), lambda i: (i // tiles_per_batch, 0))
    return pl.pallas_call(
        _xattn_kernel,
        out_shape=jax.ShapeDtypeStruct((tokens, D_MODEL), F32),
        grid=(tokens // TILE_M,),
        in_specs=[row, _const_spec((1, D_MODEL)), _const_spec((D_MODEL, D_MODEL)),
                  _const_spec((1, XA_DH)), kv, kv, _const_spec((D_MODEL, D_MODEL))],
        out_specs=row,
        compiler_params=pltpu.CompilerParams(
            dimension_semantics=("arbitrary",), vmem_limit_bytes=VMEM_LIMIT),
        name="xattn",
    )(x2d, xa_norm, wq, q_norm, k, v, wo)


def _ffn_kernel(tiles_per_batch,
                x_ref, norm_ref, w_up_ref, conv_ref, conv_b_ref, w_down_ref,
                out_ref,
                h_ref, a_ref, carry_ref, acc_ref):
    step = pl.program_id(0)

    @pl.when((step % tiles_per_batch) == 0)
    def _reset_carry():
        carry_ref[...] = jnp.zeros(carry_ref.shape, F32)

    h_ref[...] = _rms(x_ref[...], norm_ref[...]).astype(BF16)
    acc_ref[...] = x_ref[...]

    def chunk_body(j, carry):
        a_ref[0:SUBLANES, :] = carry_ref[j]
        a_ref[SUBLANES:SUBLANES + TILE_M, :] = _dot(h_ref[...], w_up_ref[j])
        carry_ref[j] = a_ref[TILE_M:TILE_M + SUBLANES, :]
        w = conv_ref[j]
        c = conv_b_ref[j]
        for tap in range(CONV_W):
            shift = CONV_W - 1 - tap
            c = c + a_ref[pl.ds(SUBLANES - shift, TILE_M), :] * w[tap:tap + 1, :]
        g = (_gelu_tanh(c[:, 0:FF_CHUNK]) * c[:, FF_CHUNK:2 * FF_CHUNK]).astype(BF16)
        acc_ref[...] += _dot(g, w_down_ref[j])
        return carry

    lax.fori_loop(0, N_FF_CHUNKS, chunk_body, 0)
    out_ref[...] = acc_ref[...]


def _ffn(x2d, norm, w_up, conv, conv_b, w_down, tiles_per_batch):
    tokens = x2d.shape[0]
    row = pl.BlockSpec((TILE_M, D_MODEL), lambda i: (i, 0))
    return pl.pallas_call(
        functools.partial(_ffn_kernel, tiles_per_batch),
        out_shape=jax.ShapeDtypeStruct((tokens, D_MODEL), F32),
        grid=(tokens // TILE_M,),
        in_specs=[row, _const_spec((1, D_MODEL)),
                  _const_spec((N_FF_CHUNKS, D_MODEL, 2 * FF_CHUNK)),
                  _const_spec((N_FF_CHUNKS, CONV_W, 2 * FF_CHUNK)),
                  _const_spec((N_FF_CHUNKS, 1, 2 * FF_CHUNK)),
                  _const_spec((N_FF_CHUNKS, FF_CHUNK, D_MODEL))],
        out_specs=row,
        scratch_shapes=[
            pltpu.VMEM((TILE_M, D_MODEL), BF16),
            pltpu.VMEM((TILE_M + SUBLANES, 2 * FF_CHUNK), F32),
            pltpu.VMEM((N_FF_CHUNKS, SUBLANES, 2 * FF_CHUNK), F32),
            pltpu.VMEM((TILE_M, D_MODEL), F32),
        ],
        compiler_params=pltpu.CompilerParams(
            dimension_semantics=("arbitrary",), vmem_limit_bytes=VMEM_LIMIT),
        name="conv_ffn",
    )(x2d, norm, w_up, conv, conv_b, w_down)


def _chunk_cols(a):
    lead = a.shape[:-1]
    a = a.reshape(lead + (2, N_FF_CHUNKS, FF_CHUNK))
    a = jnp.moveaxis(a, -2, 0)
    return a.reshape((N_FF_CHUNKS,) + lead + (2 * FF_CHUNK,))


def kernel(x, mem, positions, mix_norm, w_in, q_norm, k_norm, attn_sinks, gmlp_v_norm, gmlp_ws, gmlp_bs, attn_out_norm, gmlp_out_norm, w_out, xa_norm, mem_norm, xa_wq, xa_wkv, xa_q_norm, xa_k_norm, xa_wo, ffn_norm, ffn_up, ffn_conv, ffn_conv_b, ffn_down):
    batch, seq, d_model = x.shape
    depth = mix_norm.shape[0]
    assert d_model == D_MODEL and seq % TILE_M == 0 and mem.shape[1] == MEM_LEN
    tokens = batch * seq
    tiles_per_batch = seq // TILE_M

    inv_freq = 1.0 / (ROPE_THETA ** (jnp.arange(HALF, dtype=F32) * (2.0 / HEAD_DIM)))
    cos_t, sin_t = _rope_tables(positions, inv_freq)
    cos = cos_t.reshape(HALF, tokens).T
    sin = sin_t.reshape(HALF, tokens).T
    reps = LANES // HEAD_DIM
    cos = jnp.tile(cos, (1, 2 * reps))
    sin = jnp.tile(jnp.concatenate([-sin, sin], axis=1), (1, reps))

    x2d = x.reshape(tokens, D_MODEL)
    mem2d = mem.reshape(batch * MEM_LEN, D_MODEL)
    row = lambda a: a.reshape(1, -1)
    for l in range(depth):
        gbias = jnp.repeat(gmlp_bs[l].T, HEAD_DIM, axis=1)
        x2d = _mixer(x2d, cos, sin, attn_sinks[l], row(mix_norm[l]), w_in[l].astype(BF16),
                     row(jnp.tile(q_norm[l], reps)), row(jnp.tile(k_norm[l], reps)),
                     row(gmlp_v_norm[l]), gmlp_ws[l], gbias,
                     row(attn_out_norm[l]), row(gmlp_out_norm[l]), w_out[l].astype(BF16),
                     tiles_per_batch)
        k_mem, v_mem = _mem_kv(mem2d, row(mem_norm[l]), xa_wkv[l].astype(BF16),
                               row(xa_k_norm[l]), batch)
        x2d = _xattn(x2d, row(xa_norm[l]), xa_wq[l].astype(BF16), row(xa_q_norm[l]),
                     k_mem, v_mem, xa_wo[l].astype(BF16), tiles_per_batch)
        x2d = _ffn(x2d, row(ffn_norm[l]), _chunk_cols(ffn_up[l]).astype(BF16),
                   _chunk_cols(ffn_conv[l]), _chunk_cols(ffn_conv_b[l])[:, None, :],
                   ffn_down[l].reshape(N_FF_CHUNKS, FF_CHUNK, D_MODEL).astype(BF16),
                   tiles_per_batch)
    return x2d.reshape(batch, seq, D_MODEL)
```

```python
import functools
import math

import jax
import jax.numpy as jnp
from jax import lax
from jax.experimental import pallas as pl
from jax.experimental.pallas import tpu as pltpu

F32 = jnp.float32
BF16 = jnp.bfloat16

LANES = 128
SUBLANES = 8

D_MODEL = 1024
HEAD_DIM = 64
HALF = HEAD_DIM // 2
ATTN_W = 512
KV_W = 128
GM_W = 512
IN_COLS = ATTN_W + 2 * KV_W + 2 * GM_W
BLK = 128
ROPE_THETA = 10000.0
XA_HEADS = 4
XA_DH = 256
MEM_LEN = 256
D_FF = 2816
FF_CHUNK = 256
N_FF_CHUNKS = D_FF // FF_CHUNK
CONV_W = 3
EPS = 1e-6
MASK_BIAS = -1e30

SUB_M = 512
MIX_TILE = 2 * SUB_M
FFN_TILE = SUB_M
TILE_M = SUB_M
VMEM_LIMIT = 56 * 1024 * 1024


def _rms(x, g):
    return x * lax.rsqrt(jnp.mean(x * x, axis=-1, keepdims=True) + EPS) * g


def _gelu_tanh(x):
    c = math.sqrt(2.0 / math.pi)
    return x * (0.5 * (1.0 + jnp.tanh(c * (x + 0.044715 * (x * x * x)))))


def _dot(a, b):
    return jnp.dot(a, b, preferred_element_type=F32)


def _dot_nt(a, b):
    return lax.dot_general(a, b, (((1,), (1,)), ((), ())), preferred_element_type=F32)


def _const_spec(shape):
    nd = len(shape)
    return pl.BlockSpec(shape, lambda i: (0,) * nd, pipeline_mode=pl.Buffered(1))


def _rope_kernel(inv_freq_ref, pos_ref, cos_ref, sin_ref):
    pos = pos_ref[...].astype(F32)
    for f in range(HALF):
        ang = pos * inv_freq_ref[f]
        cos_ref[f] = jnp.cos(ang)
        sin_ref[f] = jnp.sin(ang)


def _rope_tables(positions, inv_freq):
    rows = positions.size // LANES
    pos2d = positions.reshape(rows, LANES)
    out = jax.ShapeDtypeStruct((HALF, rows, LANES), F32)
    return pl.pallas_call(
        _rope_kernel,
        out_shape=(out, out),
        in_specs=[pl.BlockSpec(memory_space=pltpu.SMEM),
                  pl.BlockSpec(memory_space=pltpu.VMEM)],
        out_specs=(pl.BlockSpec(memory_space=pltpu.VMEM),
                   pl.BlockSpec(memory_space=pltpu.VMEM)),
        compiler_params=pltpu.CompilerParams(vmem_limit_bytes=VMEM_LIMIT),
        name="rope_tables",
    )(inv_freq, pos2d)


def _mem_kv_kernel(mem_ref, mem_norm_ref, wkv_ref, k_norm_ref, k_ref, v_ref):
    m = _rms(mem_ref[...], mem_norm_ref[...]).astype(BF16)
    kv = _dot(m, wkv_ref[...])
    scale = 1.0 / math.sqrt(XA_DH)
    for h in range(XA_HEADS):
        cols = slice(h * XA_DH, (h + 1) * XA_DH)
        k_ref[:, cols] = (_rms(kv[:, cols], k_norm_ref[...]) * scale).astype(BF16)
    v_ref[...] = kv[:, D_MODEL:].astype(BF16)


def _mem_kv(mem2d, mem_norm, wkv, k_norm, batch):
    out = jax.ShapeDtypeStruct((batch * MEM_LEN, D_MODEL), BF16)
    blk = pl.BlockSpec((MEM_LEN, D_MODEL), lambda b: (b, 0))
    return pl.pallas_call(
        _mem_kv_kernel,
        out_shape=(out, out),
        grid=(batch,),
        in_specs=[blk, _const_spec((1, D_MODEL)), _const_spec((D_MODEL, 2 * D_MODEL)),
                  _const_spec((1, XA_DH))],
        out_specs=(blk, blk),
        compiler_params=pltpu.CompilerParams(
            dimension_semantics=("arbitrary",), vmem_limit_bytes=VMEM_LIMIT),
        name="mem_kv",
    )(mem2d, mem_norm, wkv, k_norm)


def _mixer_kernel(tiles_per_batch,
                  sinks_ref, x_ref, cos_ref, sin_ref, mix_norm_ref, w_in_ref,
                  q_norm_ref, k_norm_ref, gv_norm_ref, ws_ref, gbias_ref,
                  attn_norm_ref, gm_norm_ref, w_out_ref,
                  out_ref,
                  proj_ref, kbuf_ref, vbuf_ref, y_ref, wcat_ref, bias_ref):
    step = pl.program_id(0)
    first_tile = (step % tiles_per_batch) == 0
    n_blocks = MIX_TILE // BLK

    @pl.when(step == 0)
    def _build_constants():
        t = lax.broadcasted_iota(jnp.int32, (BLK, BLK), 0)
        s = lax.broadcasted_iota(jnp.int32, (BLK, BLK), 1)
        for j in range(GM_W // LANES):
            w_a = jnp.where(s <= t, ws_ref[2 * j], 0.0)
            w_b = jnp.where(s <= t, ws_ref[2 * j + 1], 0.0)
            wcat_ref[j] = jnp.concatenate([w_a, w_b], axis=1).astype(BF16)
        qi = lax.broadcasted_iota(jnp.int32, (2 * BLK, 4 * BLK), 0) & (BLK - 1)
        kj = lax.broadcasted_iota(jnp.int32, (2 * BLK, 4 * BLK), 1) & (2 * BLK - 1)
        cur = (kj >= BLK) & ((kj - BLK) <= qi)
        prev = (kj < BLK) & (kj > qi)
        bias_ref[0] = jnp.where(cur | prev, 0.0, MASK_BIAS)
        bias_ref[1] = jnp.where(cur, 0.0, MASK_BIAS)

    @pl.when(first_tile)
    def _reset_carry():
        kbuf_ref[0:BLK, :] = jnp.zeros((BLK, KV_W), F32)
        vbuf_ref[0:BLK, :] = jnp.zeros((BLK, KV_W), F32)

    for sub in range(MIX_TILE // SUB_M):
        srows = slice(sub * SUB_M, (sub + 1) * SUB_M)
        h = _rms(x_ref[srows, :], mix_norm_ref[...]).astype(BF16)
        proj_ref[srows, :] = _dot(h, w_in_ref[...])

    lane = lax.broadcasted_iota(jnp.int32, (BLK, LANES), 1)
    low_head = lane < HEAD_DIM
    first_half = (lane & (HEAD_DIM - 1)) < HALF
    r = lax.broadcasted_iota(jnp.int32, (LANES, LANES), 0)
    c = lax.broadcasted_iota(jnp.int32, (LANES, LANES), 1)
    head_ones = jnp.where((r < HEAD_DIM) == (c < HEAD_DIM), 1.0, 0.0).astype(BF16)
    lane2 = lax.broadcasted_iota(jnp.int32, (2 * BLK, LANES), 1)
    low_head2 = lane2 < HEAD_DIM

    def head_norm_rope(t, gain, cos, sin):
        ss = _dot((t * t).astype(BF16), head_ones)
        tn = t * lax.rsqrt(ss * (1.0 / HEAD_DIM) + EPS) * gain
        rot = jnp.where(first_half, pltpu.roll(tn, LANES - HALF, 1), pltpu.roll(tn, HALF, 1))
        return tn * cos + rot * sin

    for b in range(n_blocks):
        r0 = b * BLK
        rows = pl.ds(r0, BLK)
        cos = cos_ref[rows, :]
        sin = sin_ref[rows, :]

        k = head_norm_rope(proj_ref[rows, ATTN_W:ATTN_W + KV_W], k_norm_ref[...], cos, sin)
        kbuf_ref[pl.ds(r0 + BLK, BLK), :] = k
        vbuf_ref[pl.ds(r0 + BLK, BLK), :] = proj_ref[rows, ATTN_W + KV_W:ATTN_W + 2 * KV_W]
        kcat = kbuf_ref[pl.ds(r0, 2 * BLK), :]
        vcat = vbuf_ref[pl.ds(r0, 2 * BLK), :]
        kswp = pltpu.roll(kcat, HEAD_DIM, 1)
        vswp = pltpu.roll(vcat, HEAD_DIM, 1)

        bias = bias_ref[first_tile.astype(jnp.int32)] if b == 0 else bias_ref[0]

        attn_parts = []
        for g in range(2):
            k_lo, k_hi = (kcat, kswp) if g == 0 else (kswp, kcat)
            v_lo, v_hi = (vcat, vswp) if g == 0 else (vswp, vcat)
            k_big = jnp.concatenate([jnp.where(low_head2, k_lo, 0.0),
                                     jnp.where(low_head2, 0.0, k_hi)], axis=0).astype(BF16)
            v_big = jnp.concatenate([jnp.where(low_head2, v_lo, 0.0),
                                     jnp.where(low_head2, 0.0, v_hi)], axis=0).astype(BF16)
            q_pairs = []
            for pair in (2 * g, 2 * g + 1):
                qcols = slice(pair * LANES, (pair + 1) * LANES)
                q = head_norm_rope(proj_ref[rows, qcols], q_norm_ref[...], cos, sin)
                q_pairs.append((q * (1.0 / math.sqrt(HEAD_DIM))).astype(BF16))
            qs = jnp.concatenate(q_pairs, axis=0)
            s = _dot_nt(qs, k_big) + bias
            p_rows, inv_rows = [], []
            for rg in range(2):
                p_cols, inv_cols = [], []
                for half in range(2):
                    sink = sinks_ref[4 * g + 2 * rg + half]
                    sh = s[rg * BLK:(rg + 1) * BLK, half * 2 * BLK:(half + 1) * 2 * BLK]
                    m = jnp.maximum(jnp.max(sh, axis=-1, keepdims=True), sink)
                    p = jnp.exp(sh - m)
                    denom = jnp.sum(p, axis=-1, keepdims=True) + jnp.exp(sink - m)
                    p_cols.append(p.astype(BF16))
                    inv_cols.append(1.0 / denom)
                p_rows.append(jnp.concatenate(p_cols, axis=1))
                inv_rows.append(jnp.where(low_head, inv_cols[0], inv_cols[1]))
            o = _dot(jnp.concatenate(p_rows, axis=0), v_big)
            attn_parts.append(o[0:BLK] * inv_rows[0])
            attn_parts.append(o[BLK:2 * BLK] * inv_rows[1])
        attn = jnp.concatenate(attn_parts, axis=1)
        y_ref[rows, 0:ATTN_W] = _rms(attn, attn_norm_ref[...]).astype(BF16)

        gz0 = ATTN_W + 2 * KV_W
        gu = _gelu_tanh(proj_ref[rows, gz0:gz0 + GM_W])
        gv = _rms(_gelu_tanh(proj_ref[rows, gz0 + GM_W:gz0 + 2 * GM_W]), gv_norm_ref[...])
        mixed = []
        for j in range(GM_W // LANES):
            vb = gv[:, j * LANES:(j + 1) * LANES]
            v_bd = jnp.concatenate([jnp.where(low_head, vb, 0.0),
                                    jnp.where(low_head, 0.0, vb)], axis=0).astype(BF16)
            mixed.append(_dot(wcat_ref[j], v_bd))
        gm = gu * (jnp.concatenate(mixed, axis=1) + gbias_ref[...])
        y_ref[rows, ATTN_W:ATTN_W + GM_W] = _rms(gm, gm_norm_ref[...]).astype(BF16)

        if (r0 + BLK) % SUB_M == 0:
            srows = slice(r0 + BLK - SUB_M, r0 + BLK)
            out_ref[srows, :] = x_ref[srows, :] + _dot(y_ref[srows, :], w_out_ref[...])

    kbuf_ref[0:BLK, :] = kbuf_ref[MIX_TILE:MIX_TILE + BLK, :]
    vbuf_ref[0:BLK, :] = vbuf_ref[MIX_TILE:MIX_TILE + BLK, :]


def _mixer(x2d, cos, sin, sinks, mix_norm, w_in, q_norm, k_norm, gv_norm, ws, gbias,
           attn_norm, gm_norm, w_out, tiles_per_batch):
    tokens = x2d.shape[0]
    row = lambda w: pl.BlockSpec((MIX_TILE, w), lambda i: (i, 0))
    return pl.pallas_call(
        functools.partial(_mixer_kernel, tiles_per_batch),
        out_shape=jax.ShapeDtypeStruct((tokens, D_MODEL), F32),
        grid=(tokens // MIX_TILE,),
        in_specs=[pl.BlockSpec(memory_space=pltpu.SMEM),
                  row(D_MODEL), row(LANES), row(LANES),
                  _const_spec((1, D_MODEL)), _const_spec((D_MODEL, IN_COLS)),
                  _const_spec((1, LANES)), _const_spec((1, LANES)),
                  _const_spec((1, GM_W)), _const_spec((GM_W // HEAD_DIM, BLK, BLK)),
                  _const_spec((BLK, GM_W)),
                  _const_spec((1, ATTN_W)), _const_spec((1, GM_W)),
                  _const_spec((D_MODEL, D_MODEL))],
        out_specs=row(D_MODEL),
        scratch_shapes=[
            pltpu.VMEM((MIX_TILE, IN_COLS), F32),
            pltpu.VMEM((MIX_TILE + BLK, KV_W), F32),
            pltpu.VMEM((MIX_TILE + BLK, KV_W), F32),
            pltpu.VMEM((MIX_TILE, D_MODEL), BF16),
            pltpu.VMEM((GM_W // LANES, BLK, 2 * BLK), BF16),
            pltpu.VMEM((2, 2 * BLK, 4 * BLK), F32),
        ],
        compiler_params=pltpu.CompilerParams(
            dimension_semantics=("arbitrary",), vmem_limit_bytes=VMEM_LIMIT),
        name="mixer",
    )(sinks, x2d, cos, sin, mix_norm, w_in, q_norm, k_norm, gv_norm, ws, gbias,
      attn_norm, gm_norm, w_out)


def _xattn_kernel(x_ref, xa_norm_ref, wq_ref, q_norm_ref, k_ref, v_ref, wo_ref, out_ref):
    x = x_ref[...]
    h = _rms(x, xa_norm_ref[...]).astype(BF16)
    q = _dot(h, wq_ref[...])
    outs = []
    for hd in range(XA_HEADS):
        cols = slice(hd * XA_DH, (hd + 1) * XA_DH)
        qh = _rms(q[:, cols], q_norm_ref[...]).astype(BF16)
        s = _dot_nt(qh, k_ref[:, cols])
        p = jnp.exp(s - jnp.max(s, axis=-1, keepdims=True))
        inv = 1.0 / jnp.sum(p, axis=-1, keepdims=True)
        outs.append((_dot(p.astype(BF16), v_ref[:, cols]) * inv).astype(BF16))
    out_ref[...] = x + _dot(jnp.concatenate(outs, axis=1), wo_ref[...])


def _xattn(x2d, xa_norm, wq, q_norm, k, v, wo, tiles_per_batch):
    tokens = x2d.shape[0]
    row = pl.BlockSpec((TILE_M, D_MODEL), lambda i: (i, 0))
    kv = pl.BlockSpec((MEM_LEN, D_MODEL), lambda i: (i // tiles_per_batch, 0))
    return pl.pallas_call(
        _xattn_kernel,
        out_shape=jax.ShapeDtypeStruct((tokens, D_MODEL), F32),
        grid=(tokens // TILE_M,),
        in_specs=[row, _const_spec((1, D_MODEL)), _const_spec((D_MODEL, D_MODEL)),
                  _const_spec((1, XA_DH)), kv, kv, _const_spec((D_MODEL, D_MODEL))],
        out_specs=row,
        compiler_params=pltpu.CompilerParams(
            dimension_semantics=("arbitrary",), vmem_limit_bytes=VMEM_LIMIT),
        name="xattn",
    )(x2d, xa_norm, wq, q_norm, k, v, wo)


def _ffn_kernel(tiles_per_batch,
                x_ref, norm_ref, w_up_ref, conv_ref, conv_b_ref, w_down_ref,
                out_ref,
                h_ref, a_ref, carry_ref, g_ref):
    step = pl.program_id(0)

    @pl.when((step % tiles_per_batch) == 0)
    def _reset_carry():
        carry_ref[...] = jnp.zeros(carry_ref.shape, F32)

    for sub in range(FFN_TILE // SUB_M):
        srows = slice(sub * SUB_M, (sub + 1) * SUB_M)
        h_ref[srows, :] = _rms(x_ref[srows, :], norm_ref[...]).astype(BF16)
        for j in range(N_FF_CHUNKS):
            slot = j % 2
            halves = []
            for part in range(2):
                c0 = part * D_FF + j * FF_CHUNK
                cols = slice(c0, c0 + FF_CHUNK)
                a_buf = a_ref.at[sub, slot, part]
                a_buf[0:SUBLANES, :] = carry_ref[:, cols]
                a_buf[SUBLANES:SUBLANES + SUB_M, :] = _dot(h_ref[srows, :], w_up_ref[:, cols])
                carry_ref[:, cols] = a_buf[SUB_M:SUB_M + SUBLANES, :]
                c = conv_b_ref[:, cols]
                for tap in range(CONV_W):
                    shift = CONV_W - 1 - tap
                    c = c + (a_buf[pl.ds(SUBLANES - shift, SUB_M), :]
                             * conv_ref[tap:tap + 1, cols])
                halves.append(c)
            g_ref[srows, j * FF_CHUNK:(j + 1) * FF_CHUNK] = (
                _gelu_tanh(halves[0]) * halves[1]).astype(BF16)
        out_ref[srows, :] = x_ref[srows, :] + _dot(g_ref[srows, :], w_down_ref[...])


def _ffn(x2d, norm, w_up, conv, conv_b, w_down, tiles_per_batch):
    tokens = x2d.shape[0]
    row = pl.BlockSpec((FFN_TILE, D_MODEL), lambda i: (i, 0))
    return pl.pallas_call(
        functools.partial(_ffn_kernel, tiles_per_batch),
        out_shape=jax.ShapeDtypeStruct((tokens, D_MODEL), F32),
        grid=(tokens // FFN_TILE,),
        in_specs=[row, _const_spec((1, D_MODEL)),
                  _const_spec((D_MODEL, 2 * D_FF)),
                  _const_spec((CONV_W, 2 * D_FF)),
                  _const_spec((1, 2 * D_FF)),
                  _const_spec((D_FF, D_MODEL))],
        out_specs=row,
        scratch_shapes=[
            pltpu.VMEM((FFN_TILE, D_MODEL), BF16),
            pltpu.VMEM((FFN_TILE // SUB_M, 2, 2, SUB_M + SUBLANES, FF_CHUNK), F32),
            pltpu.VMEM((SUBLANES, 2 * D_FF), F32),
            pltpu.VMEM((FFN_TILE, D_FF), BF16),
        ],
        compiler_params=pltpu.CompilerParams(
            dimension_semantics=("arbitrary",), vmem_limit_bytes=VMEM_LIMIT),
        name="conv_ffn",
    )(x2d, norm, w_up, conv, conv_b, w_down)


def kernel(x, mem, positions, mix_norm, w_in, q_norm, k_norm, attn_sinks, gmlp_v_norm, gmlp_ws, gmlp_bs, attn_out_norm, gmlp_out_norm, w_out, xa_norm, mem_norm, xa_wq, xa_wkv, xa_q_norm, xa_k_norm, xa_wo, ffn_norm, ffn_up, ffn_conv, ffn_conv_b, ffn_down):
    batch, seq, d_model = x.shape
    depth = mix_norm.shape[0]
    assert d_model == D_MODEL and mem.shape[1] == MEM_LEN
    assert seq % MIX_TILE == 0 and seq % FFN_TILE == 0 and seq % TILE_M == 0
    tokens = batch * seq

    inv_freq = 1.0 / (ROPE_THETA ** (jnp.arange(HALF, dtype=F32) * (2.0 / HEAD_DIM)))
    cos_t, sin_t = _rope_tables(positions, inv_freq)
    cos = cos_t.reshape(HALF, tokens).T
    sin = sin_t.reshape(HALF, tokens).T
    reps = LANES // HEAD_DIM
    cos = jnp.tile(cos, (1, 2 * reps))
    sin = jnp.tile(jnp.concatenate([-sin, sin], axis=1), (1, reps))

    x2d = x.reshape(tokens, D_MODEL)
    mem2d = mem.reshape(batch * MEM_LEN, D_MODEL)
    row = lambda a: a.reshape(1, -1)
    for l in range(depth):
        gbias = jnp.repeat(gmlp_bs[l].T, HEAD_DIM, axis=1)
        x2d = _mixer(x2d, cos, sin, attn_sinks[l], row(mix_norm[l]), w_in[l].astype(BF16),
                     row(jnp.tile(q_norm[l], reps)), row(jnp.tile(k_norm[l], reps)),
                     row(gmlp_v_norm[l]), gmlp_ws[l], gbias,
                     row(attn_out_norm[l]), row(gmlp_out_norm[l]), w_out[l].astype(BF16),
                     seq // MIX_TILE)
        k_mem, v_mem = _mem_kv(mem2d, row(mem_norm[l]), xa_wkv[l].astype(BF16),
                               row(xa_k_norm[l]), batch)
        x2d = _xattn(x2d, row(xa_norm[l]), xa_wq[l].astype(BF16), row(xa_q_norm[l]),
                     k_mem, v_mem, xa_wo[l].astype(BF16), seq // TILE_M)
        x2d = _ffn(x2d, row(ffn_norm[l]), ffn_up[l].astype(BF16), ffn_conv[l],
                   row(ffn_conv_b[l]), ffn_down[l].astype(BF16), seq // FFN_TILE)
    return x2d.reshape(batch, seq, D_MODEL)
```

```python
import functools
import math

import jax
import jax.numpy as jnp
from jax import lax
from jax.experimental import pallas as pl
from jax.experimental.pallas import tpu as pltpu

F32 = jnp.float32
BF16 = jnp.bfloat16

LANES = 128
SUBLANES = 8

D_MODEL = 1024
HEAD_DIM = 64
HALF = HEAD_DIM // 2
ATTN_W = 512
KV_W = 128
GM_W = 512
IN_COLS = ATTN_W + 2 * KV_W + 2 * GM_W
BLK = 128
ROPE_THETA = 10000.0
XA_HEADS = 4
XA_DH = 256
MEM_LEN = 256
D_FF = 2816
FF_CHUNK = 256
N_FF_CHUNKS = D_FF // FF_CHUNK
ROPE_ROWS = 8
PIECE_COLS = 256
GZ0 =ATTN_W + 2 * KV_W
assert GZ0 % PIECE_COLS == 0
N_DOWN_PIECES = 4
DOWN_COLS = D_MODEL // N_DOWN_PIECES
FFN_DOWN_BEFORE = (2, 5, 8, 11)
CONV_W = 3
EPS = 1e-6
MASK_BIAS = -1e30

SUB_M = 512
MIX_TILE = SUB_M
FFN_TILE = SUB_M
TILE_M = SUB_M
VMEM_LIMIT = 56 * 1024 * 1024


def _rms(x, g):
    return x * lax.rsqrt(jnp.mean(x * x, axis=-1, keepdims=True) + EPS) * g


def _gelu_tanh(x):
    c = math.sqrt(2.0 / math.pi)
    return x * (0.5 * (1.0 + jnp.tanh(c * (x + 0.044715 * (x * x * x)))))


def _dot(a, b):
    return jnp.dot(a, b, preferred_element_type=F32)


def _dot_nt(a, b):
    return lax.dot_general(a, b, (((1,), (1,)), ((), ())), preferred_element_type=F32)


def _const_spec(shape):
    nd = len(shape)
    return pl.BlockSpec(shape, lambda i: (0,) * nd, pipeline_mode=pl.Buffered(1))


def _rope_kernel(inv_freq_ref, pos_ref, cos_ref, sin_ref):
    inv_freq = inv_freq_ref[...]
    for g in range(ROPE_ROWS):
        pos = pos_ref[g:g + 1, :].astype(F32)
        ang = pos * inv_freq
        c = jnp.cos(ang)
        s = jnp.sin(ang)
        reps = LANES // HEAD_DIM
        c_all = jnp.concatenate([c, c] * reps, axis=0)
        s_all = jnp.concatenate([-s, s] * reps, axis=0)
        rows = slice(g * LANES, (g + 1) * LANES)
        cos_ref[rows, :] = c_all.T
        sin_ref[rows, :] = s_all.T


def _rope_tables(positions, inv_freq):
    tokens = positions.size
    pos2d = positions.reshape(tokens // LANES, LANES)
    out = jax.ShapeDtypeStruct((tokens, LANES), F32)
    out_spec = pl.BlockSpec((ROPE_ROWS * LANES, LANES), lambda i: (i, 0))
    return pl.pallas_call(
        _rope_kernel,
        out_shape=(out, out),
        grid=(tokens // (ROPE_ROWS * LANES),),
        in_specs=[_const_spec((HALF, 1)),
                  pl.BlockSpec((ROPE_ROWS, LANES), lambda i: (i, 0))],
        out_specs=(out_spec, out_spec),
        compiler_params=pltpu.CompilerParams(dimension_semantics=("arbitrary",)),
        name="rope_tables",
    )(inv_freq.reshape(HALF, 1), pos2d)


def _mem_kv_kernel(mem_ref, mem_norm_ref, wkv_ref, k_norm_ref, k_ref, v_ref):
    m = _rms(mem_ref[...], mem_norm_ref[...]).astype(BF16)
    kv = _dot(m, wkv_ref[...])
    scale = 1.0 / math.sqrt(XA_DH)
    for h in range(XA_HEADS):
        cols = slice(h * XA_DH, (h + 1) * XA_DH)
        k_ref[:, cols] = (_rms(kv[:, cols], k_norm_ref[...]) * scale).astype(BF16)
    v_ref[...] = kv[:, D_MODEL:].astype(BF16)


def _mem_kv(mem2d, mem_norm, wkv, k_norm, batch):
    out = jax.ShapeDtypeStruct((batch * MEM_LEN, D_MODEL), BF16)
    blk = pl.BlockSpec((MEM_LEN, D_MODEL), lambda b: (b, 0))
    return pl.pallas_call(
        _mem_kv_kernel,
        out_shape=(out, out),
        grid=(batch,),
        in_specs=[blk, _const_spec((1, D_MODEL)), _const_spec((D_MODEL, 2 * D_MODEL)),
                  _const_spec((1, XA_DH))],
        out_specs=(blk, blk),
        compiler_params=pltpu.CompilerParams(
            dimension_semantics=("arbitrary",), vmem_limit_bytes=VMEM_LIMIT),
        name="mem_kv",
    )(mem2d, mem_norm, wkv, k_norm)


def _mixer_kernel(tiles_per_batch,
                  sinks_ref, xa_ref, xc_ref, cos_ref, sin_ref, mix_norm_ref, w_in_ref,
                  q_norm_ref, k_norm_ref, gv_norm_ref, ws_ref, gbias_ref,
                  attn_norm_ref, gm_norm_ref, w_out_ref,
                  out_ref,
                  proj2_ref, hy_ref,
                  kbuf_ref, vbuf_ref, wcat_ref, bias_ref):
    step = pl.program_id(0)
    first_tile = ((step + tiles_per_batch - 1) % tiles_per_batch) == 0
    n_blocks = MIX_TILE // BLK
    proj_refs = (proj2_ref.at[0], proj2_ref.at[1])
    h_plane, y_plane = 0, (1, 2)
    y_refs = (hy_ref.at[y_plane[0]], hy_ref.at[y_plane[1]])

    @pl.when(step == 0)
    def _build_constants():
        t = lax.broadcasted_iota(jnp.int32, (BLK, BLK), 0)
        s = lax.broadcasted_iota(jnp.int32, (BLK, BLK), 1)
        for j in range(GM_W // LANES):
            w_a = jnp.where(s <= t, ws_ref[2 * j], 0.0)
            w_b = jnp.where(s <= t, ws_ref[2 * j + 1], 0.0)
            wcat_ref[j] = jnp.concatenate([w_a, w_b], axis=1).astype(BF16)
        qi = lax.broadcasted_iota(jnp.int32, (2 * BLK, 4 * BLK), 0) & (BLK - 1)
        kj = lax.broadcasted_iota(jnp.int32, (2 * BLK, 4 * BLK), 1) & (2 * BLK - 1)
        cur = (kj >= BLK) & ((kj - BLK) <= qi)
        prev = (kj < BLK) & (kj > qi)
        bias_ref[0] = jnp.where(cur | prev, 0.0, MASK_BIAS)
        bias_ref[1] = jnp.where(cur, 0.0, MASK_BIAS)
        proj2_ref[1] = jnp.zeros((MIX_TILE, IN_COLS), F32)
        hy_ref[y_plane[1]] = jnp.zeros((MIX_TILE, D_MODEL), BF16)

    @pl.when(first_tile)
    def _reset_carry():
        kbuf_ref[0:BLK, :] = jnp.zeros((BLK, KV_W), F32)
        vbuf_ref[0:BLK, :] = jnp.zeros((BLK, KV_W), F32)

    hy_ref[h_plane] = _rms(xa_ref[...], mix_norm_ref[...]).astype(BF16)

    lane = lax.broadcasted_iota(jnp.int32, (BLK, LANES), 1)
    low_head = lane < HEAD_DIM
    first_half = (lane & (HEAD_DIM - 1)) < HALF
    r = lax.broadcasted_iota(jnp.int32, (LANES, LANES), 0)
    c = lax.broadcasted_iota(jnp.int32, (LANES, LANES), 1)
    head_ones = jnp.where((r < HEAD_DIM) == (c < HEAD_DIM), 1.0, 0.0).astype(BF16)
    lane2 = lax.broadcasted_iota(jnp.int32, (2 * BLK, LANES), 1)
    low_head2 = lane2 < HEAD_DIM

    def head_norm_rope(t, gain, cos, sin):
        ss = _dot((t * t).astype(BF16), head_ones)
        tn = t * lax.rsqrt(ss * (1.0 / HEAD_DIM) + EPS) * gain
        rot = jnp.where(first_half, pltpu.roll(tn, LANES - HALF, 1), pltpu.roll(tn, HALF, 1))
        return tn * cos + rot * sin

    def block(b, proj_ref, y_ref):
        r0 = b * BLK
        rows = pl.ds(r0, BLK)
        cos = cos_ref[rows, :]
        sin = sin_ref[rows, :]

        k = head_norm_rope(proj_ref[rows, ATTN_W:ATTN_W + KV_W], k_norm_ref[...], cos, sin)
        kbuf_ref[pl.ds(r0 + BLK, BLK), :] = k
        vbuf_ref[pl.ds(r0 + BLK, BLK), :] = proj_ref[rows, ATTN_W + KV_W:ATTN_W + 2 * KV_W]
        kcat = kbuf_ref[pl.ds(r0, 2 * BLK), :]
        vcat = vbuf_ref[pl.ds(r0, 2 * BLK), :]
        kswp = pltpu.roll(kcat, HEAD_DIM, 1)
        vswp = pltpu.roll(vcat, HEAD_DIM, 1)

        bias = bias_ref[first_tile.astype(jnp.int32)] if b == 0 else bias_ref[0]

        attn_parts = []
        for g in range(2):
            k_lo, k_hi = (kcat, kswp) if g == 0 else (kswp, kcat)
            v_lo, v_hi = (vcat, vswp) if g == 0 else (vswp, vcat)
            k_big = jnp.concatenate([jnp.where(low_head2, k_lo, 0.0),
                                     jnp.where(low_head2, 0.0, k_hi)], axis=0).astype(BF16)
            v_big = jnp.concatenate([jnp.where(low_head2, v_lo, 0.0),
                                     jnp.where(low_head2, 0.0, v_hi)], axis=0).astype(BF16)
            q_pairs = []
            for pair in (2 * g, 2 * g + 1):
                qcols = slice(pair * LANES, (pair + 1) * LANES)
                q = head_norm_rope(proj_ref[rows, qcols], q_norm_ref[...], cos, sin)
                q_pairs.append((q * (1.0 / math.sqrt(HEAD_DIM))).astype(BF16))
            qs = jnp.concatenate(q_pairs, axis=0)
            s = _dot_nt(qs, k_big) + bias
            yield
            p_rows, inv_rows = [], []
            for rg in range(2):
                p_cols, inv_cols = [], []
                for half in range(2):
                    sink = sinks_ref[4 * g + 2 * rg + half]
                    sh = s[rg * BLK:(rg + 1) * BLK, half * 2 * BLK:(half + 1) * 2 * BLK]
                    m = jnp.maximum(jnp.max(sh, axis=-1, keepdims=True), sink)
                    p = jnp.exp(sh - m)
                    denom = jnp.sum(p, axis=-1, keepdims=True) + jnp.exp(sink - m)
                    p_cols.append(p.astype(BF16))
                    inv_cols.append(1.0 / denom)
                p_rows.append(jnp.concatenate(p_cols, axis=1))
                inv_rows.append(jnp.where(low_head, inv_cols[0], inv_cols[1]))
            o = _dot(jnp.concatenate(p_rows, axis=0), v_big)
            attn_parts.append(o[0:BLK] * inv_rows[0])
            attn_parts.append(o[BLK:2 * BLK] * inv_rows[1])
        attn = jnp.concatenate(attn_parts, axis=1)
        y_ref[rows, 0:ATTN_W] = _rms(attn, attn_norm_ref[...]).astype(BF16)
        yield

        gu = _gelu_tanh(proj_ref[rows, GZ0:GZ0 + GM_W])
        gv = _rms(_gelu_tanh(proj_ref[rows, GZ0 + GM_W:GZ0 + 2 * GM_W]), gv_norm_ref[...])
        mixed = []
        for j in range(GM_W // LANES):
            vb = gv[:, j * LANES:(j + 1) * LANES]
            v_bd = jnp.concatenate([jnp.where(low_head, vb, 0.0),
                                    jnp.where(low_head, 0.0, vb)], axis=0).astype(BF16)
            mixed.append(_dot(wcat_ref[j], v_bd))
        gm = gu * (jnp.concatenate(mixed, axis=1) + gbias_ref[...])
        y_ref[rows, ATTN_W:ATTN_W + GM_W] = _rms(gm, gm_norm_ref[...]).astype(BF16)

    def body(cur, prev):
        def in_piece(p):
            cols = slice(p * PIECE_COLS, (p + 1) * PIECE_COLS)
            proj_refs[cur][:, cols] = _dot(hy_ref[h_plane], w_in_ref[:, cols])

        def out_piece(p):
            cols = slice(p * PIECE_COLS, (p + 1) * PIECE_COLS)
            out_ref[:, cols] = xc_ref[:, cols] + _dot(hy_ref[y_plane[prev]], w_out_ref[:, cols])

        pieces = [functools.partial(in_piece, p) for p in range(IN_COLS // PIECE_COLS)]
        for p in range(D_MODEL // PIECE_COLS):
            pieces.insert(3 * p + 2, functools.partial(out_piece, p))
        pieces.reverse()
        for b in range(n_blocks):
            if pieces:
                pieces.pop()()
            for _ in block(b, proj_refs[prev], y_refs[cur]):
                if pieces:
                    pieces.pop()()
        while pieces:
            pieces.pop()()

    for parity in range(2):
        pl.when((step % 2) == parity)(functools.partial(body, parity, 1 - parity))

    kbuf_ref[0:BLK, :] = kbuf_ref[MIX_TILE:MIX_TILE + BLK, :]
    vbuf_ref[0:BLK, :] = vbuf_ref[MIX_TILE:MIX_TILE + BLK, :]


def _mixer(x2d, cos, sin, sinks, mix_norm, w_in, q_norm, k_norm, gv_norm, ws, gbias,
           attn_norm, gm_norm, w_out, tiles_per_batch):
    tokens = x2d.shape[0]
    n_tiles = tokens // MIX_TILE

    def stage(lag, w):
        return pl.BlockSpec((MIX_TILE, w), lambda i: (jnp.clip(i - lag, 0, n_tiles - 1), 0))

    return pl.pallas_call(
        functools.partial(_mixer_kernel, tiles_per_batch),
        out_shape=jax.ShapeDtypeStruct((tokens, D_MODEL), F32),
        grid=(n_tiles + 2,),
        in_specs=[pl.BlockSpec(memory_space=pltpu.SMEM),
                  stage(0, D_MODEL), stage(2, D_MODEL), stage(1, LANES), stage(1, LANES),
                  _const_spec((1, D_MODEL)), _const_spec((D_MODEL, IN_COLS)),
                  _const_spec((1, LANES)), _const_spec((1, LANES)),
                  _const_spec((1, GM_W)), _const_spec((GM_W // HEAD_DIM, BLK, BLK)),
                  _const_spec((BLK, GM_W)),
                  _const_spec((1, ATTN_W)), _const_spec((1, GM_W)),
                  _const_spec((D_MODEL, D_MODEL))],
        out_specs=stage(2, D_MODEL),
        scratch_shapes=[
            pltpu.VMEM((2, MIX_TILE, IN_COLS), F32),
            pltpu.VMEM((3, MIX_TILE, D_MODEL), BF16),
            pltpu.VMEM((MIX_TILE + BLK, KV_W), F32),
            pltpu.VMEM((MIX_TILE + BLK, KV_W), F32),
            pltpu.VMEM((GM_W // LANES, BLK, 2 * BLK), BF16),
            pltpu.VMEM((2, 2 * BLK, 4 * BLK), F32),
        ],
        compiler_params=pltpu.CompilerParams(
            dimension_semantics=("arbitrary",), vmem_limit_bytes=VMEM_LIMIT),
        name="mixer",
    )(sinks, x2d, x2d, cos, sin, mix_norm, w_in, q_norm, k_norm,
      gv_norm, ws, gbias, attn_norm, gm_norm, w_out)


def _xattn_kernel(x_ref, xa_norm_ref, wq_ref, q_norm_ref, k_ref, v_ref, wo_ref, out_ref):
    x = x_ref[...]
    h = _rms(x, xa_norm_ref[...]).astype(BF16)
    q = _dot(h, wq_ref[...])
    outs = []
    for hd in range(XA_HEADS):
        cols = slice(hd * XA_DH, (hd + 1) * XA_DH)
        qh = _rms(q[:, cols], q_norm_ref[...]).astype(BF16)
        s = _dot_nt(qh, k_ref[:, cols])
        p = jnp.exp(s - jnp.max(s, axis=-1, keepdims=True))
        inv = 1.0 / jnp.sum(p, axis=-1, keepdims=True)
        outs.append((_dot(p.astype(BF16), v_ref[:, cols]) * inv).astype(BF16))
    out_ref[...] = x + _dot(jnp.concatenate(outs, axis=1), wo_ref[...])


def _xattn(x2d, xa_norm, wq, q_norm, k, v, wo, tiles_per_batch):
    tokens = x2d.shape[0]
    row = pl.BlockSpec((TILE_M, D_MODEL), lambda i: (i, 0))
    kv = pl.BlockSpec((MEM_LEN, D_MODEL), lambda i: (i // tiles_per_batch, 0))
    return pl.pallas_call(
        _xattn_kernel,
        out_shape=jax.ShapeDtypeStruct((tokens, D_MODEL), F32),
        grid=(tokens // TILE_M,),
        in_specs=[row, _const_spec((1, D_MODEL)), _const_spec((D_MODEL, D_MODEL)),
                  _const_spec((1, XA_DH)), kv, kv, _const_spec((D_MODEL, D_MODEL))],
        out_specs=row,
        compiler_params=pltpu.CompilerParams(
            dimension_semantics=("arbitrary",), vmem_limit_bytes=VMEM_LIMIT),
        name="xattn",
    )(x2d, xa_norm, wq, q_norm, k, v, wo)


def _ffn_kernel(n_tiles, tiles_per_batch,
                x_ref, xprev_ref, norm_ref, w_up_ref, conv_ref, conv_b_ref, w_down_ref,
                out_ref,
                h_ref, carry_ref, g0_ref, g1_ref, a00_ref, a01_ref, a10_ref, a11_ref):
    step = pl.program_id(0)
    tile = jnp.minimum(step, n_tiles - 1)
    g_refs = (g0_ref, g1_ref)
    a_refs = ((a00_ref, a01_ref), (a10_ref, a11_ref))

    @pl.when(step == 0)
    def _zero_previous_activations():
        g1_ref[...] = jnp.zeros((FFN_TILE, D_FF), BF16)

    @pl.when((tile % tiles_per_batch) == 0)
    def _reset_carry():
        carry_ref[...] = jnp.zeros(carry_ref.shape, F32)

    h_ref[...] = _rms(x_ref[...], norm_ref[...]).astype(BF16)

    def down_piece(p, prev):
        ocols = slice(p * DOWN_COLS, (p + 1) * DOWN_COLS)
        out_ref[:, ocols] = xprev_ref[:, ocols] + _dot(g_refs[prev][...], w_down_ref[:, ocols])

    def up_proj(j, part):
        c0 = part * D_FF + j * FF_CHUNK
        cols = slice(c0, c0 + FF_CHUNK)
        a_buf = a_refs[j % 2][part]
        a_buf[0:SUBLANES, :] = carry_ref[:, cols]
        a_buf[SUBLANES:SUBLANES + FFN_TILE, :] = _dot(h_ref[...], w_up_ref[:, cols])
        carry_ref[:, cols] = a_buf[FFN_TILE:FFN_TILE + SUBLANES, :]

    def conv(j, part):
        c0 = part * D_FF + j * FF_CHUNK
        cols = slice(c0, c0 + FF_CHUNK)
        c = conv_b_ref[:, cols]
        for tap in range(CONV_W):
            shift = CONV_W - 1 - tap
            c = c + (a_refs[j % 2][part][pl.ds(SUBLANES - shift, FFN_TILE), :]
                     * conv_ref[tap:tap + 1, cols])
        return c

    def body(cur, prev):
        for j in range(N_FF_CHUNKS + 1):
            for p in range(N_DOWN_PIECES):
                if FFN_DOWN_BEFORE[p] == j:
                    down_piece(p, prev)
            if j < N_FF_CHUNKS:
                up_proj(j, 0)
                gate = conv(j, 0)
                up_proj(j, 1)
                g_refs[cur][:, j * FF_CHUNK:(j + 1) * FF_CHUNK] = (
                    _gelu_tanh(gate) * conv(j, 1)).astype(BF16)

    for parity in range(2):
        pl.when((step % 2) == parity)(functools.partial(body, parity, 1 - parity))


def _ffn(x2d, norm, w_up, conv, conv_b, w_down, tiles_per_batch):
    tokens = x2d.shape[0]
    n_tiles = tokens // FFN_TILE
    cur = pl.BlockSpec((FFN_TILE, D_MODEL), lambda i: (jnp.minimum(i, n_tiles - 1), 0))
    prev = pl.BlockSpec((FFN_TILE, D_MODEL), lambda i: (jnp.maximum(i - 1, 0), 0))
    return pl.pallas_call(
        functools.partial(_ffn_kernel, n_tiles, tiles_per_batch),
        out_shape=jax.ShapeDtypeStruct((tokens, D_MODEL), F32),
        grid=(n_tiles + 1,),
        in_specs=[cur, prev, _const_spec((1, D_MODEL)),
                  _const_spec((D_MODEL, 2 * D_FF)),
                  _const_spec((CONV_W, 2 * D_FF)),
                  _const_spec((1, 2 * D_FF)),
                  _const_spec((D_FF, D_MODEL))],
        out_specs=prev,
        scratch_shapes=[
            pltpu.VMEM((FFN_TILE, D_MODEL), BF16),
            pltpu.VMEM((SUBLANES, 2 * D_FF), F32),
        ] + [pltpu.VMEM((FFN_TILE, D_FF), BF16)] * 2
          + [pltpu.VMEM((FFN_TILE + SUBLANES, FF_CHUNK), F32)] * 4,
        compiler_params=pltpu.CompilerParams(
            dimension_semantics=("arbitrary",), vmem_limit_bytes=VMEM_LIMIT),
        name="conv_ffn",
    )(x2d, x2d, norm, w_up, conv, conv_b, w_down)


def kernel(x, mem, positions, mix_norm, w_in, q_norm, k_norm, attn_sinks, gmlp_v_norm, gmlp_ws, gmlp_bs, attn_out_norm, gmlp_out_norm, w_out, xa_norm, mem_norm, xa_wq, xa_wkv, xa_q_norm, xa_k_norm, xa_wo, ffn_norm, ffn_up, ffn_conv, ffn_conv_b, ffn_down):
    batch, seq, d_model = x.shape
    depth = mix_norm.shape[0]
    assert d_model == D_MODEL and mem.shape[1] == MEM_LEN
    assert seq % MIX_TILE == 0 and seq % FFN_TILE == 0 and seq % TILE_M == 0
    tokens = batch * seq

    inv_freq = 1.0 / (ROPE_THETA ** (jnp.arange(HALF, dtype=F32) * (2.0 / HEAD_DIM)))
    cos, sin = _rope_tables(positions, inv_freq)
    reps = LANES // HEAD_DIM

    x2d = x.reshape(tokens, D_MODEL)
    mem2d = mem.reshape(batch * MEM_LEN, D_MODEL)
    row = lambda a: a.reshape(1, -1)
    for l in range(depth):
        gbias = jnp.repeat(gmlp_bs[l].T, HEAD_DIM, axis=1)
        x2d = _mixer(x2d, cos, sin, attn_sinks[l], row(mix_norm[l]), w_in[l].astype(BF16),
                     row(jnp.tile(q_norm[l], reps)), row(jnp.tile(k_norm[l], reps)),
                     row(gmlp_v_norm[l]), gmlp_ws[l], gbias,
                     row(attn_out_norm[l]), row(gmlp_out_norm[l]), w_out[l].astype(BF16),
                     seq // MIX_TILE)
        k_mem, v_mem = _mem_kv(mem2d, row(mem_norm[l]), xa_wkv[l].astype(BF16),
                               row(xa_k_norm[l]), batch)
        x2d = _xattn(x2d, row(xa_norm[l]), xa_wq[l].astype(BF16), row(xa_q_norm[l]),
                     k_mem, v_mem, xa_wo[l].astype(BF16), seq // TILE_M)
        x2d = _ffn(x2d, row(ffn_norm[l]), ffn_up[l].astype(BF16), ffn_conv[l],
                   row(ffn_conv_b[l]), ffn_down[l].astype(BF16), seq // FFN_TILE)
    return x2d.reshape(batch, seq, D_MODEL)
```

```python
import functools
import math

import jax
import jax.numpy as jnp
from jax import lax
from jax.experimental import pallas as pl
from jax.experimental.pallas import tpu as pltpu

F32 = jnp.float32
BF16 = jnp.bfloat16

LANES = 128
SUBLANES = 8

D_MODEL = 1024
HEAD_DIM = 64
HALF = HEAD_DIM // 2
ATTN_W = 512
KV_W = 128
GM_W = 512
IN_COLS = ATTN_W + 2 * KV_W + 2 * GM_W
BLK = 128
ROPE_THETA = 10000.0
XA_HEADS = 4
XA_DH = 256
MEM_LEN = 256
D_FF = 2816
FF_CHUNK = 256
N_FF_CHUNKS = D_FF // FF_CHUNK
ROPE_ROWS = 8
PIECE_COLS = 256
GZ0 = ATTN_W + 2 * KV_W
CONV_W = 3
EPS = 1e-6
MASK_BIAS = -1e30

SUB_M = 512
MIX_TILE = SUB_M
FFN_TILE = SUB_M
TILE_M = SUB_M
VMEM_LIMIT = 56 * 1024 * 1024


def _rms(x, g):
    return x * lax.rsqrt(jnp.mean(x * x, axis=-1, keepdims=True) + EPS) * g


def _gelu_tanh(x):
    c = math.sqrt(2.0 / math.pi)
    return x * (0.5 * (1.0 + jnp.tanh(c * (x + 0.044715 * (x * x * x)))))


def _dot(a, b):
    return jnp.dot(a, b, preferred_element_type=F32)


def _dot_nt(a, b):
    return lax.dot_general(a, b, (((1,), (1,)), ((), ())), preferred_element_type=F32)


def _const_spec(shape):
    nd = len(shape)
    return pl.BlockSpec(shape, lambda i: (0,) * nd, pipeline_mode=pl.Buffered(1))


def _rope_kernel(inv_freq_ref, pos_ref, cos_ref, sin_ref):
    inv_freq = inv_freq_ref[...]
    for g in range(ROPE_ROWS):
        pos = pos_ref[g:g + 1, :].astype(F32)
        ang = pos * inv_freq
        c = jnp.cos(ang)
        s = jnp.sin(ang)
        reps = LANES // HEAD_DIM
        c_all = jnp.concatenate([c, c] * reps, axis=0)
        s_all = jnp.concatenate([-s, s] * reps, axis=0)
        rows = slice(g * LANES, (g + 1) * LANES)
        cos_ref[rows, :] = c_all.T
        sin_ref[rows, :] = s_all.T


def _rope_tables(positions, inv_freq):
    tokens = positions.size
    pos2d = positions.reshape(tokens // LANES, LANES)
    out = jax.ShapeDtypeStruct((tokens, LANES), F32)
    out_spec = pl.BlockSpec((ROPE_ROWS * LANES, LANES), lambda i: (i, 0))
    return pl.pallas_call(
        _rope_kernel,
        out_shape=(out, out),
        grid=(tokens // (ROPE_ROWS * LANES),),
        in_specs=[_const_spec((HALF, 1)),
                  pl.BlockSpec((ROPE_ROWS, LANES), lambda i: (i, 0))],
        out_specs=(out_spec, out_spec),
        compiler_params=pltpu.CompilerParams(dimension_semantics=("arbitrary",)),
        name="rope_tables",
    )(inv_freq.reshape(HALF, 1), pos2d)


def _mem_kv_kernel(mem_ref, mem_norm_ref, wkv_ref, k_norm_ref, k_ref, v_ref):
    m = _rms(mem_ref[...], mem_norm_ref[...]).astype(BF16)
    kv = _dot(m, wkv_ref[...])
    scale = 1.0 / math.sqrt(XA_DH)
    for h in range(XA_HEADS):
        cols = slice(h * XA_DH, (h + 1) * XA_DH)
        k_ref[:, cols] = (_rms(kv[:, cols], k_norm_ref[...]) * scale).astype(BF16)
    v_ref[...] = kv[:, D_MODEL:].astype(BF16)


def _mem_kv(mem2d, mem_norm, wkv, k_norm, batch):
    out = jax.ShapeDtypeStruct((batch * MEM_LEN, D_MODEL), BF16)
    blk = pl.BlockSpec((MEM_LEN, D_MODEL), lambda b: (b, 0))
    return pl.pallas_call(
        _mem_kv_kernel,
        out_shape=(out, out),
        grid=(batch,),
        in_specs=[blk, _const_spec((1, D_MODEL)), _const_spec((D_MODEL, 2 * D_MODEL)),
                  _const_spec((1, XA_DH))],
        out_specs=(blk, blk),
        compiler_params=pltpu.CompilerParams(
            dimension_semantics=("arbitrary",), vmem_limit_bytes=VMEM_LIMIT),
        name="mem_kv",
    )(mem2d, mem_norm, wkv, k_norm)


def _mixer_kernel(tiles_per_batch,
                  sinks_ref, xa_ref, xc_ref, cos_ref, sin_ref, mix_norm_ref, w_in_ref,
                  q_norm_ref, k_norm_ref, gv_norm_ref, ws_ref, gbias_ref,
                  attn_norm_ref, gm_norm_ref, w_out_ref,
                  out_ref,
                  proj2_ref, hy_ref,
                  kbuf_ref, vbuf_ref, wcat_ref, bias_ref):
    step = pl.program_id(0)
    first_tile = ((step + tiles_per_batch - 1) % tiles_per_batch) == 0
    n_blocks = MIX_TILE // BLK
    proj_refs = (proj2_ref.at[0], proj2_ref.at[1])
    h_plane, y_plane = 0, (1, 2)
    y_refs = (hy_ref.at[y_plane[0]], hy_ref.at[y_plane[1]])

    @pl.when(step == 0)
    def _build_constants():
        t = lax.broadcasted_iota(jnp.int32, (BLK, BLK), 0)
        s = lax.broadcasted_iota(jnp.int32, (BLK, BLK), 1)
        for j in range(GM_W // LANES):
            w_a = jnp.where(s <= t, ws_ref[2 * j], 0.0)
            w_b = jnp.where(s <= t, ws_ref[2 * j + 1], 0.0)
            wcat_ref[j] = jnp.concatenate([w_a, w_b], axis=1).astype(BF16)
        qi = lax.broadcasted_iota(jnp.int32, (2 * BLK, 4 * BLK), 0) & (BLK - 1)
        kj = lax.broadcasted_iota(jnp.int32, (2 * BLK, 4 * BLK), 1) & (2 * BLK - 1)
        cur = (kj >= BLK) & ((kj - BLK) <= qi)
        prev = (kj < BLK) & (kj > qi)
        bias_ref[0] = jnp.where(cur | prev, 0.0, MASK_BIAS)
        bias_ref[1] = jnp.where(cur, 0.0, MASK_BIAS)
        proj2_ref[1] = jnp.zeros((MIX_TILE, IN_COLS), F32)
        hy_ref[y_plane[1]] = jnp.zeros((MIX_TILE, D_MODEL), BF16)

    @pl.when(first_tile)
    def _reset_carry():
        kbuf_ref[0:BLK, :] = jnp.zeros((BLK, KV_W), F32)
        vbuf_ref[0:BLK, :] = jnp.zeros((BLK, KV_W), F32)

    hy_ref[h_plane] = _rms(xa_ref[...], mix_norm_ref[...]).astype(BF16)

    lane = lax.broadcasted_iota(jnp.int32, (BLK, LANES), 1)
    low_head = lane < HEAD_DIM
    first_half = (lane & (HEAD_DIM - 1)) < HALF
    r = lax.broadcasted_iota(jnp.int32, (LANES, LANES), 0)
    c = lax.broadcasted_iota(jnp.int32, (LANES, LANES), 1)
    head_ones = jnp.where((r < HEAD_DIM) == (c < HEAD_DIM), 1.0, 0.0).astype(BF16)
    lane2 = lax.broadcasted_iota(jnp.int32, (2 * BLK, LANES), 1)
    low_head2 = lane2 < HEAD_DIM

    def head_norm_rope(t, gain, cos, sin):
        ss = _dot((t * t).astype(BF16), head_ones)
        tn = t * lax.rsqrt(ss * (1.0 / HEAD_DIM) + EPS) * gain
        rot = jnp.where(first_half, pltpu.roll(tn, LANES - HALF, 1), pltpu.roll(tn, HALF, 1))
        return tn * cos + rot * sin

    def block(b, proj_ref, y_ref):
        r0 = b * BLK
        rows = pl.ds(r0, BLK)
        cos = cos_ref[rows, :]
        sin = sin_ref[rows, :]

        k = head_norm_rope(proj_ref[rows, ATTN_W:ATTN_W + KV_W], k_norm_ref[...], cos, sin)
        kbuf_ref[pl.ds(r0 + BLK, BLK), :] = k
        vbuf_ref[pl.ds(r0 + BLK, BLK), :] = proj_ref[rows, ATTN_W + KV_W:ATTN_W + 2 * KV_W]
        kcat = kbuf_ref[pl.ds(r0, 2 * BLK), :]
        vcat = vbuf_ref[pl.ds(r0, 2 * BLK), :]
        kswp = pltpu.roll(kcat, HEAD_DIM, 1)
        vswp = pltpu.roll(vcat, HEAD_DIM, 1)

        bias = bias_ref[first_tile.astype(jnp.int32)] if b == 0 else bias_ref[0]

        attn_parts = []
        for g in range(2):
            k_lo, k_hi = (kcat, kswp) if g == 0 else (kswp, kcat)
            v_lo, v_hi = (vcat, vswp) if g == 0 else (vswp, vcat)
            k_big = jnp.concatenate([jnp.where(low_head2, k_lo, 0.0),
                                     jnp.where(low_head2, 0.0, k_hi)], axis=0).astype(BF16)
            v_big = jnp.concatenate([jnp.where(low_head2, v_lo, 0.0),
                                     jnp.where(low_head2, 0.0, v_hi)], axis=0).astype(BF16)
            q_pairs = []
            for pair in (2 * g, 2 * g + 1):
                qcols = slice(pair * LANES, (pair + 1) * LANES)
                q = head_norm_rope(proj_ref[rows, qcols], q_norm_ref[...], cos, sin)
                q_pairs.append((q * (1.0 / math.sqrt(HEAD_DIM))).astype(BF16))
            qs = jnp.concatenate(q_pairs, axis=0)
            s = _dot_nt(qs, k_big) + bias
            yield
            p_rows, inv_rows = [], []
            for rg in range(2):
                p_cols, inv_cols = [], []
                for half in range(2):
                    sink = sinks_ref[4 * g + 2 * rg + half]
                    sh = s[rg * BLK:(rg + 1) * BLK, half * 2 * BLK:(half + 1) * 2 * BLK]
                    m = jnp.maximum(jnp.max(sh, axis=-1, keepdims=True), sink)
                    p = jnp.exp(sh - m)
                    denom = jnp.sum(p, axis=-1, keepdims=True) + jnp.exp(sink - m)
                    p_cols.append(p.astype(BF16))
                    inv_cols.append(1.0 / denom)
                p_rows.append(jnp.concatenate(p_cols, axis=1))
                inv_rows.append(jnp.where(low_head, inv_cols[0], inv_cols[1]))
            o = _dot(jnp.concatenate(p_rows, axis=0), v_big)
            attn_parts.append(o[0:BLK] * inv_rows[0])
            attn_parts.append(o[BLK:2 * BLK] * inv_rows[1])
        attn = jnp.concatenate(attn_parts, axis=1)
        y_ref[rows, 0:ATTN_W] = _rms(attn, attn_norm_ref[...]).astype(BF16)
        yield

        gu = _gelu_tanh(proj_ref[rows, GZ0:GZ0 + GM_W])
        gv = _rms(_gelu_tanh(proj_ref[rows, GZ0 + GM_W:GZ0 + 2 * GM_W]), gv_norm_ref[...])
        mixed = []
        for j in range(GM_W // LANES):
            vb = gv[:, j * LANES:(j + 1) * LANES]
            v_bd = jnp.concatenate([jnp.where(low_head, vb, 0.0),
                                    jnp.where(low_head, 0.0, vb)], axis=0).astype(BF16)
            mixed.append(_dot(wcat_ref[j], v_bd))
        gm = gu * (jnp.concatenate(mixed, axis=1) + gbias_ref[...])
        y_ref[rows, ATTN_W:ATTN_W + GM_W] = _rms(gm, gm_norm_ref[...]).astype(BF16)

    def body(cur, prev):
        def in_piece(p):
            cols = slice(p * PIECE_COLS, (p + 1) * PIECE_COLS)
            proj_refs[cur][:, cols] = _dot(hy_ref[h_plane], w_in_ref[:, cols])

        def out_piece(p):
            cols = slice(p * PIECE_COLS, (p + 1) * PIECE_COLS)
            out_ref[:, cols] = xc_ref[:, cols] + _dot(hy_ref[y_plane[prev]], w_out_ref[:, cols])

        pieces = [functools.partial(in_piece, p) for p in range(IN_COLS // PIECE_COLS)]
        for p in range(D_MODEL // PIECE_COLS):
            pieces.insert(3 * p + 2, functools.partial(out_piece, p))
        pieces.reverse()
        for b in range(n_blocks):
            if pieces:
                pieces.pop()()
            for _ in block(b, proj_refs[prev], y_refs[cur]):
                if pieces:
                    pieces.pop()()
        while pieces:
            pieces.pop()()

    for parity in range(2):
        pl.when((step % 2) == parity)(functools.partial(body, parity, 1 - parity))

    kbuf_ref[0:BLK, :] = kbuf_ref[MIX_TILE:MIX_TILE + BLK, :]
    vbuf_ref[0:BLK, :] = vbuf_ref[MIX_TILE:MIX_TILE + BLK, :]


def _mixer(x2d, cos, sin, sinks, mix_norm, w_in, q_norm, k_norm, gv_norm, ws, gbias,
           attn_norm, gm_norm, w_out, tiles_per_batch):
    tokens = x2d.shape[0]
    n_tiles = tokens // MIX_TILE

    def stage(lag, w):
        return pl.BlockSpec((MIX_TILE, w), lambda i: (jnp.clip(i - lag, 0, n_tiles - 1), 0))

    return pl.pallas_call(
        functools.partial(_mixer_kernel, tiles_per_batch),
        out_shape=jax.ShapeDtypeStruct((tokens, D_MODEL), F32),
        grid=(n_tiles + 2,),
        in_specs=[pl.BlockSpec(memory_space=pltpu.SMEM),
                  stage(0, D_MODEL), stage(2, D_MODEL), stage(1, LANES), stage(1, LANES),
                  _const_spec((1, D_MODEL)), _const_spec((D_MODEL, IN_COLS)),
                  _const_spec((1, LANES)), _const_spec((1, LANES)),
                  _const_spec((1, GM_W)), _const_spec((GM_W // HEAD_DIM, BLK, BLK)),
                  _const_spec((BLK, GM_W)),
                  _const_spec((1, ATTN_W)), _const_spec((1, GM_W)),
                  _const_spec((D_MODEL, D_MODEL))],
        out_specs=stage(2, D_MODEL),
        scratch_shapes=[
            pltpu.VMEM((2, MIX_TILE, IN_COLS), F32),
            pltpu.VMEM((3, MIX_TILE, D_MODEL), BF16),
            pltpu.VMEM((MIX_TILE + BLK, KV_W), F32),
            pltpu.VMEM((MIX_TILE + BLK, KV_W), F32),
            pltpu.VMEM((GM_W // LANES, BLK, 2 * BLK), BF16),
            pltpu.VMEM((2, 2 * BLK, 4 * BLK), F32),
        ],
        compiler_params=pltpu.CompilerParams(
            dimension_semantics=("arbitrary",), vmem_limit_bytes=VMEM_LIMIT),
        name="mixer",
    )(sinks, x2d, x2d, cos, sin, mix_norm, w_in, q_norm, k_norm,
      gv_norm, ws, gbias, attn_norm, gm_norm, w_out)


def _xattn_kernel(x_ref, xa_norm_ref, wq_ref, q_norm_ref, k_ref, v_ref, wo_ref, out_ref):
    x = x_ref[...]
    h = _rms(x, xa_norm_ref[...]).astype(BF16)
    q = _dot(h, wq_ref[...])
    outs = []
    for hd in range(XA_HEADS):
        cols = slice(hd * XA_DH, (hd + 1) * XA_DH)
        qh = _rms(q[:, cols], q_norm_ref[...]).astype(BF16)
        s = _dot_nt(qh, k_ref[:, cols])
        p = jnp.exp(s - jnp.max(s, axis=-1, keepdims=True))
        inv = 1.0 / jnp.sum(p, axis=-1, keepdims=True)
        outs.append((_dot(p.astype(BF16), v_ref[:, cols]) * inv).astype(BF16))
    out_ref[...] = x + _dot(jnp.concatenate(outs, axis=1), wo_ref[...])


def _xattn(x2d, xa_norm, wq, q_norm, k, v, wo, tiles_per_batch):
    tokens = x2d.shape[0]
    row = pl.BlockSpec((TILE_M, D_MODEL), lambda i: (i, 0))
    kv = pl.BlockSpec((MEM_LEN, D_MODEL), lambda i: (i // tiles_per_batch, 0))
    return pl.pallas_call(
        _xattn_kernel,
        out_shape=jax.ShapeDtypeStruct((tokens, D_MODEL), F32),
        grid=(tokens // TILE_M,),
        in_specs=[row, _const_spec((1, D_MODEL)), _const_spec((D_MODEL, D_MODEL)),
                  _const_spec((1, XA_DH)), kv, kv, _const_spec((D_MODEL, D_MODEL))],
        out_specs=row,
        compiler_params=pltpu.CompilerParams(
            dimension_semantics=("arbitrary",), vmem_limit_bytes=VMEM_LIMIT),
        name="xattn",
    )(x2d, xa_norm, wq, q_norm, k, v, wo)


def _ffn_kernel(tiles_per_batch,
                x_ref, norm_ref, w_up_ref, conv_ref, conv_b_ref, w_down_ref,
                out_ref,
                h_ref, carry_ref, g_ref, a00_ref, a01_ref, a10_ref, a11_ref):
    step = pl.program_id(0)
    a_refs = ((a00_ref, a01_ref), (a10_ref, a11_ref))

    @pl.when((step % tiles_per_batch) == 0)
    def _reset_carry():
        carry_ref[...] = jnp.zeros(carry_ref.shape, F32)

    h_ref[...] = _rms(x_ref[...], norm_ref[...]).astype(BF16)

    def up_proj(j, part):
        c0 = part * D_FF + j * FF_CHUNK
        cols = slice(c0, c0 + FF_CHUNK)
        a_buf = a_refs[j % 2][part]
        a_buf[0:SUBLANES, :] = carry_ref[:, cols]
        a_buf[SUBLANES:SUBLANES + FFN_TILE, :] = _dot(h_ref[...], w_up_ref[:, cols])
        carry_ref[:, cols] = a_buf[FFN_TILE:FFN_TILE + SUBLANES, :]

    def conv(j, part):
        c0 = part * D_FF + j * FF_CHUNK
        cols = slice(c0, c0 + FF_CHUNK)
        a_full = a_refs[j % 2][part][...]
        c = conv_b_ref[:, cols]
        for tap in range(CONV_W):
            shift = CONV_W - 1 - tap
            shifted = pltpu.roll(a_full, shift, 0) if shift else a_full
            c = c + shifted[SUBLANES:, :] * conv_ref[tap:tap + 1, cols]
        return c

    for j in range(N_FF_CHUNKS):
        up_proj(j, 0)
        gate = conv(j, 0)
        up_proj(j, 1)
        g_ref[:, j * FF_CHUNK:(j + 1) * FF_CHUNK] = (
            _gelu_tanh(gate) * conv(j, 1)).astype(BF16)
    out_ref[...] = x_ref[...] + _dot(g_ref[...], w_down_ref[...])


def _ffn(x2d, norm, w_up, conv, conv_b, w_down, tiles_per_batch):
    tokens = x2d.shape[0]
    row = pl.BlockSpec((FFN_TILE, D_MODEL), lambda i: (i, 0))
    return pl.pallas_call(
        functools.partial(_ffn_kernel, tiles_per_batch),
        out_shape=jax.ShapeDtypeStruct((tokens, D_MODEL), F32),
        grid=(tokens // FFN_TILE,),
        in_specs=[row, _const_spec((1, D_MODEL)),
                  _const_spec((D_MODEL, 2 * D_FF)),
                  _const_spec((CONV_W, 2 * D_FF)),
                  _const_spec((1, 2 * D_FF)),
                  _const_spec((D_FF, D_MODEL))],
        out_specs=row,
        scratch_shapes=[
            pltpu.VMEM((FFN_TILE, D_MODEL), BF16),
            pltpu.VMEM((SUBLANES, 2 * D_FF), F32),
            pltpu.VMEM((FFN_TILE, D_FF), BF16),
        ] + [pltpu.VMEM((FFN_TILE + SUBLANES, FF_CHUNK), F32)] * 4,
        compiler_params=pltpu.CompilerParams(
            dimension_semantics=("arbitrary",), vmem_limit_bytes=VMEM_LIMIT),
        name="conv_ffn",
    )(x2d, norm, w_up, conv, conv_b, w_down)


def kernel(x, mem, positions, mix_norm, w_in, q_norm, k_norm, attn_sinks, gmlp_v_norm, gmlp_ws, gmlp_bs, attn_out_norm, gmlp_out_norm, w_out, xa_norm, mem_norm, xa_wq, xa_wkv, xa_q_norm, xa_k_norm, xa_wo, ffn_norm, ffn_up, ffn_conv, ffn_conv_b, ffn_down):
    batch, seq, d_model = x.shape
    depth = mix_norm.shape[0]
    assert d_model == D_MODEL and mem.shape[1] == MEM_LEN
    assert seq % MIX_TILE == 0 and seq % FFN_TILE == 0 and seq % TILE_M == 0
    tokens = batch * seq

    inv_freq = 1.0 / (ROPE_THETA ** (jnp.arange(HALF, dtype=F32) * (2.0 / HEAD_DIM)))
    cos, sin = _rope_tables(positions, inv_freq)
    reps = LANES // HEAD_DIM

    x2d = x.reshape(tokens, D_MODEL)
    mem2d = mem.reshape(batch * MEM_LEN, D_MODEL)
    row = lambda a: a.reshape(1, -1)
    for l in range(depth):
        gbias = jnp.repeat(gmlp_bs[l].T, HEAD_DIM, axis=1)
        x2d = _mixer(x2d, cos, sin, attn_sinks[l], row(mix_norm[l]), w_in[l].astype(BF16),
                     row(jnp.tile(q_norm[l], reps)), row(jnp.tile(k_norm[l], reps)),
                     row(gmlp_v_norm[l]), gmlp_ws[l], gbias,
                     row(attn_out_norm[l]), row(gmlp_out_norm[l]), w_out[l].astype(BF16),
                     seq // MIX_TILE)
        k_mem, v_mem = _mem_kv(mem2d, row(mem_norm[l]), xa_wkv[l].astype(BF16),
                               row(xa_k_norm[l]), batch)
        x2d = _xattn(x2d, row(xa_norm[l]), xa_wq[l].astype(BF16), row(xa_q_norm[l]),
                     k_mem, v_mem, xa_wo[l].astype(BF16), seq // TILE_M)
        x2d = _ffn(x2d, row(ffn_norm[l]), ffn_up[l].astype(BF16), ffn_conv[l],
                   row(ffn_conv_b[l]), ffn_down[l].astype(BF16), seq // FFN_TILE)
    return x2d.reshape(batch, seq, D_MODEL)
```

```python
import functools
import math

import jax
import jax.numpy as jnp
from jax import lax
from jax.experimental import pallas as pl
from jax.experimental.pallas import tpu as pltpu

F32 = jnp.float32
BF16 = jnp.bfloat16

LANES = 128
SUBLANES = 8

D_MODEL = 1024
HEAD_DIM = 64
HALF = HEAD_DIM // 2
ATTN_W = 512
KV_W = 128
GM_W = 512
IN_COLS = ATTN_W + 2 * KV_W + 2 * GM_W
BLK = 128
ROPE_THETA = 10000.0
XA_HEADS = 4
XA_DH = 256
MEM_LEN = 256
D_FF = 2816
FF_CHUNK = 256
N_FF_CHUNKS = D_FF // FF_CHUNK
ROPE_ROWS = 8
PIECE_COLS = 256
GZ0 = ATTN_W + 2 * KV_W
CONV_W = 3
EPS = 1e-6
MASK_BIAS = -1e30

SUB_M = 512
MIX_TILE = SUB_M
FFN_TILE = 2 * SUB_M
TILE_M = 2 * SUB_M
VMEM_LIMIT = 56 * 1024 * 1024


def _rms(x, g):
    return x * lax.rsqrt(jnp.mean(x * x, axis=-1, keepdims=True) + EPS) * g


def _gelu_tanh(x):
    c = math.sqrt(2.0 / math.pi)
    return x * (0.5 * (1.0 + jnp.tanh(c * (x + 0.044715 * (x * x * x)))))


def _dot(a, b):
    return jnp.dot(a, b, preferred_element_type=F32)


def _dot_nt(a, b):
    return lax.dot_general(a, b, (((1,), (1,)), ((), ())), preferred_element_type=F32)


def _const_spec(shape):
    nd = len(shape)
    return pl.BlockSpec(shape, lambda i: (0,) * nd, pipeline_mode=pl.Buffered(1))


def _rope_kernel(inv_freq_ref, pos_ref, cos_ref, sin_ref):
    inv_freq = inv_freq_ref[...]
    for g in range(ROPE_ROWS):
        pos = pos_ref[g:g + 1, :].astype(F32)
        ang = pos * inv_freq
        c = jnp.cos(ang)
        s = jnp.sin(ang)
        reps = LANES // HEAD_DIM
        c_all = jnp.concatenate([c, c] * reps, axis=0)
        s_all = jnp.concatenate([-s, s] * reps, axis=0)
        rows = slice(g * LANES, (g + 1) * LANES)
        cos_ref[rows, :] = c_all.T
        sin_ref[rows, :] = s_all.T


def _rope_tables(positions, inv_freq):
    tokens = positions.size
    pos2d = positions.reshape(tokens // LANES, LANES)
    out = jax.ShapeDtypeStruct((tokens, LANES), F32)
    out_spec = pl.BlockSpec((ROPE_ROWS * LANES, LANES), lambda i: (i, 0))
    return pl.pallas_call(
        _rope_kernel,
        out_shape=(out, out),
        grid=(tokens // (ROPE_ROWS * LANES),),
        in_specs=[_const_spec((HALF, 1)),
                  pl.BlockSpec((ROPE_ROWS, LANES), lambda i: (i, 0))],
        out_specs=(out_spec, out_spec),
        compiler_params=pltpu.CompilerParams(dimension_semantics=("arbitrary",)),
        name="rope_tables",
    )(inv_freq.reshape(HALF, 1), pos2d)


def _mem_kv_kernel(mem_ref, mem_norm_ref, wkv_ref, k_norm_ref, k_ref, v_ref):
    m = _rms(mem_ref[...], mem_norm_ref[...]).astype(BF16)
    kv = _dot(m, wkv_ref[...])
    scale = 1.0 / math.sqrt(XA_DH)
    for h in range(XA_HEADS):
        cols = slice(h * XA_DH, (h + 1) * XA_DH)
        k_ref[:, cols] = (_rms(kv[:, cols], k_norm_ref[...]) * scale).astype(BF16)
    v_ref[...] = kv[:, D_MODEL:].astype(BF16)


def _mem_kv(mem2d, mem_norm, wkv, k_norm, batch):
    out = jax.ShapeDtypeStruct((batch * MEM_LEN, D_MODEL), BF16)
    blk = pl.BlockSpec((MEM_LEN, D_MODEL), lambda b: (b, 0))
    return pl.pallas_call(
        _mem_kv_kernel,
        out_shape=(out, out),
        grid=(batch,),
        in_specs=[blk, _const_spec((1, D_MODEL)), _const_spec((D_MODEL, 2 * D_MODEL)),
                  _const_spec((1, XA_DH))],
        out_specs=(blk, blk),
        compiler_params=pltpu.CompilerParams(
            dimension_semantics=("arbitrary",), vmem_limit_bytes=VMEM_LIMIT),
        name="mem_kv",
    )(mem2d, mem_norm, wkv, k_norm)


def _mixer_kernel(tiles_per_batch,
                  sinks_ref, xa_ref, xc_ref, cos_ref, sin_ref, mix_norm_ref, w_in_ref,
                  q_norm_ref, k_norm_ref, gv_norm_ref, ws_ref, gbias_ref,
                  attn_norm_ref, gm_norm_ref, w_out_ref,
                  out_ref,
                  proj2_ref, hy_ref,
                  kbuf_ref, vbuf_ref, wcat_ref, bias_ref):
    step = pl.program_id(0)
    first_tile = ((step + tiles_per_batch - 1) % tiles_per_batch) == 0
    n_blocks = MIX_TILE // BLK
    proj_refs = (proj2_ref.at[0], proj2_ref.at[1])
    h_plane, y_plane = 0, (1, 2)
    y_refs = (hy_ref.at[y_plane[0]], hy_ref.at[y_plane[1]])

    @pl.when(step == 0)
    def _build_constants():
        t = lax.broadcasted_iota(jnp.int32, (BLK, BLK), 0)
        s = lax.broadcasted_iota(jnp.int32, (BLK, BLK), 1)
        for j in range(GM_W // LANES):
            w_a = jnp.where(s <= t, ws_ref[2 * j], 0.0)
            w_b = jnp.where(s <= t, ws_ref[2 * j + 1], 0.0)
            wcat_ref[j] = jnp.concatenate([w_a, w_b], axis=1).astype(BF16)
        qi = lax.broadcasted_iota(jnp.int32, (2 * BLK, 4 * BLK), 0) & (BLK - 1)
        kj = lax.broadcasted_iota(jnp.int32, (2 * BLK, 4 * BLK), 1) & (2 * BLK - 1)
        cur = (kj >= BLK) & ((kj - BLK) <= qi)
        prev = (kj < BLK) & (kj > qi)
        bias_ref[0] = jnp.where(cur | prev, 0.0, MASK_BIAS)
        bias_ref[1] = jnp.where(cur, 0.0, MASK_BIAS)
        proj2_ref[1] = jnp.zeros((MIX_TILE, IN_COLS), F32)
        hy_ref[y_plane[1]] = jnp.zeros((MIX_TILE, D_MODEL), BF16)

    @pl.when(first_tile)
    def _reset_carry():
        kbuf_ref[0:BLK, :] = jnp.zeros((BLK, KV_W), F32)
        vbuf_ref[0:BLK, :] = jnp.zeros((BLK, KV_W), F32)

    hy_ref[h_plane] = _rms(xa_ref[...], mix_norm_ref[...]).astype(BF16)

    lane = lax.broadcasted_iota(jnp.int32, (BLK, LANES), 1)
    low_head = lane < HEAD_DIM
    first_half = (lane & (HEAD_DIM - 1)) < HALF
    r = lax.broadcasted_iota(jnp.int32, (LANES, LANES), 0)
    c = lax.broadcasted_iota(jnp.int32, (LANES, LANES), 1)
    head_ones = jnp.where((r < HEAD_DIM) == (c < HEAD_DIM), 1.0, 0.0).astype(BF16)
    lane2 = lax.broadcasted_iota(jnp.int32, (2 * BLK, LANES), 1)
    low_head2 = lane2 < HEAD_DIM

    def head_norm_rope(t, gain, cos, sin):
        ss = _dot((t * t).astype(BF16), head_ones)
        tn = t * lax.rsqrt(ss * (1.0 / HEAD_DIM) + EPS) * gain
        rot = jnp.where(first_half, pltpu.roll(tn, LANES - HALF, 1), pltpu.roll(tn, HALF, 1))
        return tn * cos + rot * sin

    def block(b, proj_ref, y_ref):
        r0 = b * BLK
        rows = pl.ds(r0, BLK)
        cos = cos_ref[rows, :]
        sin = sin_ref[rows, :]

        k = head_norm_rope(proj_ref[rows, ATTN_W:ATTN_W + KV_W], k_norm_ref[...], cos, sin)
        kbuf_ref[pl.ds(r0 + BLK, BLK), :] = k
        vbuf_ref[pl.ds(r0 + BLK, BLK), :] = proj_ref[rows, ATTN_W + KV_W:ATTN_W + 2 * KV_W]
        kcat = kbuf_ref[pl.ds(r0, 2 * BLK), :]
        vcat = vbuf_ref[pl.ds(r0, 2 * BLK), :]
        kswp = pltpu.roll(kcat, HEAD_DIM, 1)
        vswp = pltpu.roll(vcat, HEAD_DIM, 1)

        bias = bias_ref[first_tile.astype(jnp.int32)] if b == 0 else bias_ref[0]

        attn_parts = []
        for g in range(2):
            k_lo, k_hi = (kcat, kswp) if g == 0 else (kswp, kcat)
            v_lo, v_hi = (vcat, vswp) if g == 0 else (vswp, vcat)
            k_big = jnp.concatenate([jnp.where(low_head2, k_lo, 0.0),
                                     jnp.where(low_head2, 0.0, k_hi)], axis=0).astype(BF16)
            v_big = jnp.concatenate([jnp.where(low_head2, v_lo, 0.0),
                                     jnp.where(low_head2, 0.0, v_hi)], axis=0).astype(BF16)
            q_pairs = []
            for pair in (2 * g, 2 * g + 1):
                qcols = slice(pair * LANES, (pair + 1) * LANES)
                q = head_norm_rope(proj_ref[rows, qcols], q_norm_ref[...], cos, sin)
                q_pairs.append((q * (1.0 / math.sqrt(HEAD_DIM))).astype(BF16))
            qs = jnp.concatenate(q_pairs, axis=0)
            s = _dot_nt(qs, k_big) + bias
            yield
            p_rows, inv_rows = [], []
            for rg in range(2):
                p_cols, inv_cols = [], []
                for half in range(2):
                    sink = sinks_ref[4 * g + 2 * rg + half]
                    sh = s[rg * BLK:(rg + 1) * BLK, half * 2 * BLK:(half + 1) * 2 * BLK]
                    m = jnp.maximum(jnp.max(sh, axis=-1, keepdims=True), sink)
                    p = jnp.exp(sh - m)
                    denom = jnp.sum(p, axis=-1, keepdims=True) + jnp.exp(sink - m)
                    p_cols.append(p.astype(BF16))
                    inv_cols.append(1.0 / denom)
                p_rows.append(jnp.concatenate(p_cols, axis=1))
                inv_rows.append(jnp.where(low_head, inv_cols[0], inv_cols[1]))
            o = _dot(jnp.concatenate(p_rows, axis=0), v_big)
            attn_parts.append(o[0:BLK] * inv_rows[0])
            attn_parts.append(o[BLK:2 * BLK] * inv_rows[1])
        attn = jnp.concatenate(attn_parts, axis=1)
        y_ref[rows, 0:ATTN_W] = _rms(attn, attn_norm_ref[...]).astype(BF16)
        yield

        gu = _gelu_tanh(proj_ref[rows, GZ0:GZ0 + GM_W])
        gv = _rms(_gelu_tanh(proj_ref[rows, GZ0 + GM_W:GZ0 + 2 * GM_W]), gv_norm_ref[...])
        mixed = []
        for j in range(GM_W // LANES):
            vb = gv[:, j * LANES:(j + 1) * LANES]
            v_bd = jnp.concatenate([jnp.where(low_head, vb, 0.0),
                                    jnp.where(low_head, 0.0, vb)], axis=0).astype(BF16)
            mixed.append(_dot(wcat_ref[j], v_bd))
        gm = gu * (jnp.concatenate(mixed, axis=1) + gbias_ref[...])
        y_ref[rows, ATTN_W:ATTN_W + GM_W] = _rms(gm, gm_norm_ref[...]).astype(BF16)

    def body(cur, prev):
        def in_piece(p):
            cols = slice(p * PIECE_COLS, (p + 1) * PIECE_COLS)
            proj_refs[cur][:, cols] = _dot(hy_ref[h_plane], w_in_ref[:, cols])

        def out_piece(p):
            cols = slice(p * PIECE_COLS, (p + 1) * PIECE_COLS)
            out_ref[:, cols] = xc_ref[:, cols] + _dot(hy_ref[y_plane[prev]], w_out_ref[:, cols])

        pieces = [functools.partial(in_piece, p) for p in range(IN_COLS // PIECE_COLS)]
        for p in range(D_MODEL // PIECE_COLS):
            pieces.insert(3 * p + 2, functools.partial(out_piece, p))
        pieces.reverse()
        for b in range(n_blocks):
            if pieces:
                pieces.pop()()
            for _ in block(b, proj_refs[prev], y_refs[cur]):
                if pieces:
                    pieces.pop()()
        while pieces:
            pieces.pop()()

    for parity in range(2):
        pl.when((step % 2) == parity)(functools.partial(body, parity, 1 - parity))

    kbuf_ref[0:BLK, :] = kbuf_ref[MIX_TILE:MIX_TILE + BLK, :]
    vbuf_ref[0:BLK, :] = vbuf_ref[MIX_TILE:MIX_TILE + BLK, :]


def _mixer(x2d, cos, sin, sinks, mix_norm, w_in, q_norm, k_norm, gv_norm, ws, gbias,
           attn_norm, gm_norm, w_out, tiles_per_batch):
    tokens = x2d.shape[0]
    n_tiles = tokens // MIX_TILE

    def stage(lag, w):
        return pl.BlockSpec((MIX_TILE, w), lambda i: (jnp.clip(i - lag, 0, n_tiles - 1), 0))

    return pl.pallas_call(
        functools.partial(_mixer_kernel, tiles_per_batch),
        out_shape=jax.ShapeDtypeStruct((tokens, D_MODEL), F32),
        grid=(n_tiles + 2,),
        in_specs=[pl.BlockSpec(memory_space=pltpu.SMEM),
                  stage(0, D_MODEL), stage(2, D_MODEL), stage(1, LANES), stage(1, LANES),
                  _const_spec((1, D_MODEL)), _const_spec((D_MODEL, IN_COLS)),
                  _const_spec((1, LANES)), _const_spec((1, LANES)),
                  _const_spec((1, GM_W)), _const_spec((GM_W // HEAD_DIM, BLK, BLK)),
                  _const_spec((BLK, GM_W)),
                  _const_spec((1, ATTN_W)), _const_spec((1, GM_W)),
                  _const_spec((D_MODEL, D_MODEL))],
        out_specs=stage(2, D_MODEL),
        scratch_shapes=[
            pltpu.VMEM((2, MIX_TILE, IN_COLS), F32),
            pltpu.VMEM((3, MIX_TILE, D_MODEL), BF16),
            pltpu.VMEM((MIX_TILE + BLK, KV_W), F32),
            pltpu.VMEM((MIX_TILE + BLK, KV_W), F32),
            pltpu.VMEM((GM_W // LANES, BLK, 2 * BLK), BF16),
            pltpu.VMEM((2, 2 * BLK, 4 * BLK), F32),
        ],
        compiler_params=pltpu.CompilerParams(
            dimension_semantics=("arbitrary",), vmem_limit_bytes=VMEM_LIMIT),
        name="mixer",
    )(sinks, x2d, x2d, cos, sin, mix_norm, w_in, q_norm, k_norm,
      gv_norm, ws, gbias, attn_norm, gm_norm, w_out)


def _xattn_kernel(x_ref, xa_norm_ref, wq_ref, q_norm_ref, k_ref, v_ref, wo_ref, out_ref):
    x = x_ref[...]
    h = _rms(x, xa_norm_ref[...]).astype(BF16)
    q = _dot(h, wq_ref[...])
    outs = []
    for hd in range(XA_HEADS):
        cols = slice(hd * XA_DH, (hd + 1) * XA_DH)
        qh = _rms(q[:, cols], q_norm_ref[...]).astype(BF16)
        s = _dot_nt(qh, k_ref[:, cols])
        p = jnp.exp(s - jnp.max(s, axis=-1, keepdims=True))
        inv = 1.0 / jnp.sum(p, axis=-1, keepdims=True)
        outs.append((_dot(p.astype(BF16), v_ref[:, cols]) * inv).astype(BF16))
    out_ref[...] = x + _dot(jnp.concatenate(outs, axis=1), wo_ref[...])


def _xattn(x2d, xa_norm, wq, q_norm, k, v, wo, tiles_per_batch):
    tokens = x2d.shape[0]
    row = pl.BlockSpec((TILE_M, D_MODEL), lambda i: (i, 0))
    kv = pl.BlockSpec((MEM_LEN, D_MODEL), lambda i: (i // tiles_per_batch, 0))
    return pl.pallas_call(
        _xattn_kernel,
        out_shape=jax.ShapeDtypeStruct((tokens, D_MODEL), F32),
        grid=(tokens // TILE_M,),
        in_specs=[row, _const_spec((1, D_MODEL)), _const_spec((D_MODEL, D_MODEL)),
                  _const_spec((1, XA_DH)), kv, kv, _const_spec((D_MODEL, D_MODEL))],
        out_specs=row,
        compiler_params=pltpu.CompilerParams(
            dimension_semantics=("arbitrary",), vmem_limit_bytes=VMEM_LIMIT),
        name="xattn",
    )(x2d, xa_norm, wq, q_norm, k, v, wo)


def _ffn_kernel(tiles_per_batch,
                x_ref, norm_ref, w_up_ref, conv_ref, conv_b_ref, w_down_ref,
                out_ref,
                h_ref, carry_ref, g_ref, a00_ref, a01_ref, a10_ref, a11_ref):
    step = pl.program_id(0)
    a_refs = ((a00_ref, a01_ref), (a10_ref, a11_ref))

    @pl.when((step % tiles_per_batch) == 0)
    def _reset_carry():
        carry_ref[...] = jnp.zeros(carry_ref.shape, F32)

    h_ref[...] = _rms(x_ref[...], norm_ref[...]).astype(BF16)

    def up_proj(j, part):
        c0 = part * D_FF + j * FF_CHUNK
        cols = slice(c0, c0 + FF_CHUNK)
        a_buf = a_refs[j % 2][part]
        a_buf[0:SUBLANES, :] = carry_ref[:, cols]
        a_buf[SUBLANES:SUBLANES + FFN_TILE, :] = _dot(h_ref[...], w_up_ref[:, cols])
        carry_ref[:, cols] = a_buf[FFN_TILE:FFN_TILE + SUBLANES, :]

    def conv(j, part):
        c0 = part * D_FF + j * FF_CHUNK
        cols = slice(c0, c0 + FF_CHUNK)
        a_full = a_refs[j % 2][part][...]
        c = conv_b_ref[:, cols]
        for tap in range(CONV_W):
            shift = CONV_W - 1 - tap
            shifted = pltpu.roll(a_full, shift, 0) if shift else a_full
            c = c + shifted[SUBLANES:, :] * conv_ref[tap:tap + 1, cols]
        return c

    for j in range(N_FF_CHUNKS):
        up_proj(j, 0)
        gate = conv(j, 0)
        up_proj(j, 1)
        g_ref[:, j * FF_CHUNK:(j + 1) * FF_CHUNK] = (
            _gelu_tanh(gate) * conv(j, 1)).astype(BF16)
    out_ref[...] = x_ref[...] + _dot(g_ref[...], w_down_ref[...])


def _ffn(x2d, norm, w_up, conv, conv_b, w_down, tiles_per_batch):
    tokens = x2d.shape[0]
    row = pl.BlockSpec((FFN_TILE, D_MODEL), lambda i: (i, 0))
    return pl.pallas_call(
        functools.partial(_ffn_kernel, tiles_per_batch),
        out_shape=jax.ShapeDtypeStruct((tokens, D_MODEL), F32),
        grid=(tokens // FFN_TILE,),
        in_specs=[row, _const_spec((1, D_MODEL)),
                  _const_spec((D_MODEL, 2 * D_FF)),
                  _const_spec((CONV_W, 2 * D_FF)),
                  _const_spec((1, 2 * D_FF)),
                  _const_spec((D_FF, D_MODEL))],
        out_specs=row,
        scratch_shapes=[
            pltpu.VMEM((FFN_TILE, D_MODEL), BF16),
            pltpu.VMEM((SUBLANES, 2 * D_FF), F32),
            pltpu.VMEM((FFN_TILE, D_FF), BF16),
        ] + [pltpu.VMEM((FFN_TILE + SUBLANES, FF_CHUNK), F32)] * 4,
        compiler_params=pltpu.CompilerParams(
            dimension_semantics=("arbitrary",), vmem_limit_bytes=VMEM_LIMIT),
        name="conv_ffn",
    )(x2d, norm, w_up, conv, conv_b, w_down)


def kernel(x, mem, positions, mix_norm, w_in, q_norm, k_norm, attn_sinks, gmlp_v_norm, gmlp_ws, gmlp_bs, attn_out_norm, gmlp_out_norm, w_out, xa_norm, mem_norm, xa_wq, xa_wkv, xa_q_norm, xa_k_norm, xa_wo, ffn_norm, ffn_up, ffn_conv, ffn_conv_b, ffn_down):
    batch, seq, d_model = x.shape
    depth = mix_norm.shape[0]
    assert d_model == D_MODEL and mem.shape[1] == MEM_LEN
    assert seq % MIX_TILE == 0 and seq % FFN_TILE == 0 and seq % TILE_M == 0
    tokens = batch * seq

    inv_freq = 1.0 / (ROPE_THETA ** (jnp.arange(HALF, dtype=F32) * (2.0 / HEAD_DIM)))
    cos, sin = _rope_tables(positions, inv_freq)
    reps = LANES // HEAD_DIM

    x2d = x.reshape(tokens, D_MODEL)
    mem2d = mem.reshape(batch * MEM_LEN, D_MODEL)
    row = lambda a: a.reshape(1, -1)
    for l in range(depth):
        gbias = jnp.repeat(gmlp_bs[l].T, HEAD_DIM, axis=1)
        x2d = _mixer(x2d, cos, sin, attn_sinks[l], row(mix_norm[l]), w_in[l].astype(BF16),
                     row(jnp.tile(q_norm[l], reps)), row(jnp.tile(k_norm[l], reps)),
                     row(gmlp_v_norm[l]), gmlp_ws[l], gbias,
                     row(attn_out_norm[l]), row(gmlp_out_norm[l]), w_out[l].astype(BF16),
                     seq // MIX_TILE)
        k_mem, v_mem = _mem_kv(mem2d, row(mem_norm[l]), xa_wkv[l].astype(BF16),
                               row(xa_k_norm[l]), batch)
        x2d = _xattn(x2d, row(xa_norm[l]), xa_wq[l].astype(BF16), row(xa_q_norm[l]),
                     k_mem, v_mem, xa_wo[l].astype(BF16), seq // TILE_M)
        x2d = _ffn(x2d, row(ffn_norm[l]), ffn_up[l].astype(BF16), ffn_conv[l],
                   row(ffn_conv_b[l]), ffn_down[l].astype(BF16), seq // FFN_TILE)
    return x2d.reshape(batch, seq, D_MODEL)
```

```python
import functools
import math

import jax
import jax.numpy as jnp
from jax import lax
from jax.experimental import pallas as pl
from jax.experimental.pallas import tpu as pltpu

F32 = jnp.float32
BF16 = jnp.bfloat16

LANES = 128
SUBLANES = 8

D_MODEL = 1024
HEAD_DIM = 64
HALF = HEAD_DIM // 2
ATTN_W = 512
KV_W = 128
GM_W = 512
IN_COLS = ATTN_W + 2 * KV_W + 2 * GM_W
BLK = 128
ROPE_THETA = 10000.0
XA_HEADS = 4
XA_DH = 256
MEM_LEN = 256
D_FF = 2816
FF_CHUNK = 256
N_FF_CHUNKS = D_FF // FF_CHUNK
ROPE_ROWS = 8
PIECE_COLS = 256
GZ0 = ATTN_W + 2 * KV_W
CONV_W = 3
EPS = 1e-6
MASK_BIAS = -1e30

SUB_M = 512
MIX_TILE = SUB_M
FFN_TILE = 2 * SUB_M
TILE_M = 2 * SUB_M
VMEM_LIMIT = 56 * 1024 * 1024


def _rms(x, g):
    return x * lax.rsqrt(jnp.mean(x * x, axis=-1, keepdims=True) + EPS) * g


def _gelu_tanh(x):
    c = math.sqrt(2.0 / math.pi)
    return x * (0.5 * (1.0 + jnp.tanh(c * (x + 0.044715 * (x * x * x)))))


def _dot(a, b):
    return jnp.dot(a, b, preferred_element_type=F32)


def _dot_nt(a, b):
    return lax.dot_general(a, b, (((1,), (1,)), ((), ())), preferred_element_type=F32)


def _const_spec(shape):
    nd = len(shape)
    return pl.BlockSpec(shape, lambda i: (0,) * nd, pipeline_mode=pl.Buffered(1))


def _rope_kernel(inv_freq_ref, pos_ref, cos_ref, sin_ref):
    inv_freq = inv_freq_ref[...]
    for g in range(ROPE_ROWS):
        pos = pos_ref[g:g + 1, :].astype(F32)
        ang = pos * inv_freq
        c = jnp.cos(ang)
        s = jnp.sin(ang)
        reps = LANES // HEAD_DIM
        c_all = jnp.concatenate([c, c] * reps, axis=0)
        s_all = jnp.concatenate([-s, s] * reps, axis=0)
        rows = slice(g * LANES, (g + 1) * LANES)
        cos_ref[rows, :] = c_all.T
        sin_ref[rows, :] = s_all.T


def _rope_tables(positions, inv_freq):
    tokens = positions.size
    pos2d = positions.reshape(tokens // LANES, LANES)
    out = jax.ShapeDtypeStruct((tokens, LANES), F32)
    out_spec = pl.BlockSpec((ROPE_ROWS * LANES, LANES), lambda i: (i, 0))
    return pl.pallas_call(
        _rope_kernel,
        out_shape=(out, out),
        grid=(tokens // (ROPE_ROWS * LANES),),
        in_specs=[_const_spec((HALF, 1)),
                  pl.BlockSpec((ROPE_ROWS, LANES), lambda i: (i, 0))],
        out_specs=(out_spec, out_spec),
        compiler_params=pltpu.CompilerParams(dimension_semantics=("arbitrary",)),
        name="rope_tables",
    )(inv_freq.reshape(HALF, 1), pos2d)


def _mem_kv_kernel(mem_ref, mem_norm_ref, wkv_ref, k_norm_ref, k_ref, v_ref):
    m = _rms(mem_ref[...], mem_norm_ref[...]).astype(BF16)
    kv = _dot(m, wkv_ref[...])
    scale = 1.0 / math.sqrt(XA_DH)
    for h in range(XA_HEADS):
        cols = slice(h * XA_DH, (h + 1) * XA_DH)
        k_ref[:, cols] = (_rms(kv[:, cols], k_norm_ref[...]) * scale).astype(BF16)
    v_ref[...] = kv[:, D_MODEL:].astype(BF16)


def _mem_kv(mem2d, mem_norm, wkv, k_norm, batch):
    out = jax.ShapeDtypeStruct((batch * MEM_LEN, D_MODEL), BF16)
    blk = pl.BlockSpec((MEM_LEN, D_MODEL), lambda b: (b, 0))
    return pl.pallas_call(
        _mem_kv_kernel,
        out_shape=(out, out),
        grid=(batch,),
        in_specs=[blk, _const_spec((1, D_MODEL)), _const_spec((D_MODEL, 2 * D_MODEL)),
                  _const_spec((1, XA_DH))],
        out_specs=(blk, blk),
        compiler_params=pltpu.CompilerParams(
            dimension_semantics=("arbitrary",), vmem_limit_bytes=VMEM_LIMIT),
        name="mem_kv",
    )(mem2d, mem_norm, wkv, k_norm)


def _mixer_kernel(tiles_per_batch,
                  sinks_ref, xa_ref, xc_ref, cos_ref, sin_ref, mix_norm_ref, w_in_ref,
                  q_norm_ref, k_norm_ref, gv_norm_ref, ws_ref, gbias_ref,
                  attn_norm_ref, gm_norm_ref, w_out_ref,
                  out_ref,
                  proj2_ref, hy_ref,
                  kbuf_ref, vbuf_ref, wcat_ref, bias_ref):
    step = pl.program_id(0)
    first_tile = ((step + tiles_per_batch - 1) % tiles_per_batch) == 0
    n_blocks = MIX_TILE // BLK
    proj_refs = (proj2_ref.at[0], proj2_ref.at[1])
    h_plane, y_plane = 0, (1, 2)
    y_refs = (hy_ref.at[y_plane[0]], hy_ref.at[y_plane[1]])

    @pl.when(step == 0)
    def _build_constants():
        t = lax.broadcasted_iota(jnp.int32, (BLK, BLK), 0)
        s = lax.broadcasted_iota(jnp.int32, (BLK, BLK), 1)
        for j in range(GM_W // LANES):
            w_a = jnp.where(s <= t, ws_ref[2 * j], 0.0)
            w_b = jnp.where(s <= t, ws_ref[2 * j + 1], 0.0)
            wcat_ref[j] = jnp.concatenate([w_a, w_b], axis=1).astype(BF16)
        qi = lax.broadcasted_iota(jnp.int32, (2 * BLK, 4 * BLK), 0) & (BLK - 1)
        kj = lax.broadcasted_iota(jnp.int32, (2 * BLK, 4 * BLK), 1) & (2 * BLK - 1)
        cur = (kj >= BLK) & ((kj - BLK) <= qi)
        prev = (kj < BLK) & (kj > qi)
        bias_ref[0] = jnp.where(cur | prev, 0.0, MASK_BIAS)
        bias_ref[1] = jnp.where(cur, 0.0, MASK_BIAS)
        proj2_ref[1] = jnp.zeros((MIX_TILE, IN_COLS), F32)
        hy_ref[y_plane[1]] = jnp.zeros((MIX_TILE, D_MODEL), BF16)

    @pl.when(first_tile)
    def _reset_carry():
        kbuf_ref[0:BLK, :] = jnp.zeros((BLK, KV_W), F32)
        vbuf_ref[0:BLK, :] = jnp.zeros((BLK, KV_W), F32)

    hy_ref[h_plane] = _rms(xa_ref[...], mix_norm_ref[...]).astype(BF16)

    lane = lax.broadcasted_iota(jnp.int32, (BLK, LANES), 1)
    low_head = lane < HEAD_DIM
    first_half = (lane & (HEAD_DIM - 1)) < HALF
    r = lax.broadcasted_iota(jnp.int32, (LANES, LANES), 0)
    c = lax.broadcasted_iota(jnp.int32, (LANES, LANES), 1)
    head_ones = jnp.where((r < HEAD_DIM) == (c < HEAD_DIM), 1.0, 0.0).astype(BF16)
    lane2 = lax.broadcasted_iota(jnp.int32, (2 * BLK, LANES), 1)
    low_head2 = lane2 < HEAD_DIM

    def head_norm_rope(t, gain, cos, sin):
        ss = _dot((t * t).astype(BF16), head_ones)
        tn = t * lax.rsqrt(ss * (1.0 / HEAD_DIM) + EPS) * gain
        rot = jnp.where(first_half, pltpu.roll(tn, LANES - HALF, 1), pltpu.roll(tn, HALF, 1))
        return tn * cos + rot * sin

    def block(b, proj_ref, y_ref):
        r0 = b * BLK
        rows = pl.ds(r0, BLK)
        cos = cos_ref[rows, :]
        sin = sin_ref[rows, :]

        k = head_norm_rope(proj_ref[rows, ATTN_W:ATTN_W + KV_W], k_norm_ref[...], cos, sin)
        kbuf_ref[pl.ds(r0 + BLK, BLK), :] = k
        vbuf_ref[pl.ds(r0 + BLK, BLK), :] = proj_ref[rows, ATTN_W + KV_W:ATTN_W + 2 * KV_W]
        kcat = kbuf_ref[pl.ds(r0, 2 * BLK), :]
        vcat = vbuf_ref[pl.ds(r0, 2 * BLK), :]
        kswp = pltpu.roll(kcat, HEAD_DIM, 1)
        vswp = pltpu.roll(vcat, HEAD_DIM, 1)

        bias = bias_ref[first_tile.astype(jnp.int32)] if b == 0 else bias_ref[0]

        attn_parts = []
        for g in range(2):
            k_lo, k_hi = (kcat, kswp) if g == 0 else (kswp, kcat)
            v_lo, v_hi = (vcat, vswp) if g == 0 else (vswp, vcat)
            k_big = jnp.concatenate([jnp.where(low_head2, k_lo, 0.0),
                                     jnp.where(low_head2, 0.0, k_hi)], axis=0).astype(BF16)
            v_big = jnp.concatenate([jnp.where(low_head2, v_lo, 0.0),
                                     jnp.where(low_head2, 0.0, v_hi)], axis=0).astype(BF16)
            q_pairs = []
            for pair in (2 * g, 2 * g + 1):
                qcols = slice(pair * LANES, (pair + 1) * LANES)
                q = head_norm_rope(proj_ref[rows, qcols], q_norm_ref[...], cos, sin)
                q_pairs.append((q * (1.0 / math.sqrt(HEAD_DIM))).astype(BF16))
            qs = jnp.concatenate(q_pairs, axis=0)
            s = _dot_nt(qs, k_big) + bias
            yield
            p_rows, inv_rows = [], []
            for rg in range(2):
                p_cols, inv_cols = [], []
                for half in range(2):
                    sink = sinks_ref[4 * g + 2 * rg + half]
                    sh = s[rg * BLK:(rg + 1) * BLK, half * 2 * BLK:(half + 1) * 2 * BLK]
                    m = jnp.maximum(jnp.max(sh, axis=-1, keepdims=True), sink)
                    p = jnp.exp(sh - m)
                    denom = jnp.sum(p, axis=-1, keepdims=True) + jnp.exp(sink - m)
                    p_cols.append(p.astype(BF16))
                    inv_cols.append(1.0 / denom)
                p_rows.append(jnp.concatenate(p_cols, axis=1))
                inv_rows.append(jnp.where(low_head, inv_cols[0], inv_cols[1]))
            o = _dot(jnp.concatenate(p_rows, axis=0), v_big)
            attn_parts.append(o[0:BLK] * inv_rows[0])
            attn_parts.append(o[BLK:2 * BLK] * inv_rows[1])
        attn = jnp.concatenate(attn_parts, axis=1)
        y_ref[rows, 0:ATTN_W] = _rms(attn, attn_norm_ref[...]).astype(BF16)
        yield

        gu = _gelu_tanh(proj_ref[rows, GZ0:GZ0 + GM_W])
        gv = _rms(_gelu_tanh(proj_ref[rows, GZ0 + GM_W:GZ0 + 2 * GM_W]), gv_norm_ref[...])
        mixed = []
        for j in range(GM_W // LANES):
            vb = gv[:, j * LANES:(j + 1) * LANES]
            v_bd = jnp.concatenate([jnp.where(low_head, vb, 0.0),
                                    jnp.where(low_head, 0.0, vb)], axis=0).astype(BF16)
            mixed.append(_dot(wcat_ref[j], v_bd))
        gm = gu * (jnp.concatenate(mixed, axis=1) + gbias_ref[...])
        y_ref[rows, ATTN_W:ATTN_W + GM_W] = _rms(gm, gm_norm_ref[...]).astype(BF16)

    def body(cur, prev):
        def in_piece(p):
            cols = slice(p * PIECE_COLS, (p + 1) * PIECE_COLS)
            proj_refs[cur][:, cols] = _dot(hy_ref[h_plane], w_in_ref[:, cols])

        def out_piece(p):
            cols = slice(p * PIECE_COLS, (p + 1) * PIECE_COLS)
            out_ref[:, cols] = xc_ref[:, cols] + _dot(hy_ref[y_plane[prev]], w_out_ref[:, cols])

        pieces = [functools.partial(in_piece, p) for p in range(IN_COLS // PIECE_COLS)]
        for p in range(D_MODEL // PIECE_COLS):
            pieces.insert(3 * p + 2, functools.partial(out_piece, p))
        pieces.reverse()
        for b in range(n_blocks):
            if pieces:
                pieces.pop()()
            for _ in block(b, proj_refs[prev], y_refs[cur]):
                if pieces:
                    pieces.pop()()
        while pieces:
            pieces.pop()()

    for parity in range(2):
        pl.when((step % 2) == parity)(functools.partial(body, parity, 1 - parity))

    kbuf_ref[0:BLK, :] = kbuf_ref[MIX_TILE:MIX_TILE + BLK, :]
    vbuf_ref[0:BLK, :] = vbuf_ref[MIX_TILE:MIX_TILE + BLK, :]


def _mixer(x2d, cos, sin, sinks, mix_norm, w_in, q_norm, k_norm, gv_norm, ws, gbias,
           attn_norm, gm_norm, w_out, tiles_per_batch):
    tokens = x2d.shape[0]
    n_tiles = tokens // MIX_TILE

    def stage(lag, w):
        return pl.BlockSpec((MIX_TILE, w), lambda i: (jnp.clip(i - lag, 0, n_tiles - 1), 0))

    return pl.pallas_call(
        functools.partial(_mixer_kernel, tiles_per_batch),
        out_shape=jax.ShapeDtypeStruct((tokens, D_MODEL), F32),
        grid=(n_tiles + 2,),
        in_specs=[pl.BlockSpec(memory_space=pltpu.SMEM),
                  stage(0, D_MODEL), stage(2, D_MODEL), stage(1, LANES), stage(1, LANES),
                  _const_spec((1, D_MODEL)), _const_spec((D_MODEL, IN_COLS)),
                  _const_spec((1, LANES)), _const_spec((1, LANES)),
                  _const_spec((1, GM_W)), _const_spec((GM_W // HEAD_DIM, BLK, BLK)),
                  _const_spec((BLK, GM_W)),
                  _const_spec((1, ATTN_W)), _const_spec((1, GM_W)),
                  _const_spec((D_MODEL, D_MODEL))],
        out_specs=stage(2, D_MODEL),
        scratch_shapes=[
            pltpu.VMEM((2, MIX_TILE, IN_COLS), F32),
            pltpu.VMEM((3, MIX_TILE, D_MODEL), BF16),
            pltpu.VMEM((MIX_TILE + BLK, KV_W), F32),
            pltpu.VMEM((MIX_TILE + BLK, KV_W), F32),
            pltpu.VMEM((GM_W // LANES, BLK, 2 * BLK), BF16),
            pltpu.VMEM((2, 2 * BLK, 4 * BLK), F32),
        ],
        compiler_params=pltpu.CompilerParams(
            dimension_semantics=("arbitrary",), vmem_limit_bytes=VMEM_LIMIT),
        name="mixer",
    )(sinks, x2d, x2d, cos, sin, mix_norm, w_in, q_norm, k_norm,
      gv_norm, ws, gbias, attn_norm, gm_norm, w_out)


def _xattn_kernel(x_ref, xa_norm_ref, wq_ref, q_norm_ref, k_ref, v_ref, wo_ref, out_ref):
    x = x_ref[...]
    h = _rms(x, xa_norm_ref[...]).astype(BF16)
    q = _dot(h, wq_ref[...])
    outs = []
    for hd in range(XA_HEADS):
        cols = slice(hd * XA_DH, (hd + 1) * XA_DH)
        qh = _rms(q[:, cols], q_norm_ref[...]).astype(BF16)
        s = _dot_nt(qh, k_ref[:, cols])
        p = jnp.exp(s - jnp.max(s, axis=-1, keepdims=True))
        inv = 1.0 / jnp.sum(p, axis=-1, keepdims=True)
        outs.append((_dot(p.astype(BF16), v_ref[:, cols]) * inv).astype(BF16))
    out_ref[...] = x + _dot(jnp.concatenate(outs, axis=1), wo_ref[...])


def _xattn(x2d, xa_norm, wq, q_norm, k, v, wo, tiles_per_batch):
    tokens = x2d.shape[0]
    row = pl.BlockSpec((TILE_M, D_MODEL), lambda i: (i, 0))
    kv = pl.BlockSpec((MEM_LEN, D_MODEL), lambda i: (i // tiles_per_batch, 0))
    return pl.pallas_call(
        _xattn_kernel,
        out_shape=jax.ShapeDtypeStruct((tokens, D_MODEL), F32),
        grid=(tokens // TILE_M,),
        in_specs=[row, _const_spec((1, D_MODEL)), _const_spec((D_MODEL, D_MODEL)),
                  _const_spec((1, XA_DH)), kv, kv, _const_spec((D_MODEL, D_MODEL))],
        out_specs=row,
        compiler_params=pltpu.CompilerParams(
            dimension_semantics=("arbitrary",), vmem_limit_bytes=VMEM_LIMIT),
        name="xattn",
    )(x2d, xa_norm, wq, q_norm, k, v, wo)


def _ffn_kernel(tiles_per_batch,
                x_ref, norm_ref, w_up_ref, conv_ref, conv_b_ref, w_down_ref,
                out_ref,
                xp_ref, op_ref, h_ref, carry_ref, g_ref):
    step = pl.program_id(0)
    nb = FFN_TILE // SUBLANES

    @pl.when((step % tiles_per_batch) == 0)
    def _reset_carry():
        carry_ref[...] = jnp.zeros(carry_ref.shape, F32)

    n_lane_groups = D_MODEL // LANES
    pitch = nb + SUBLANES
    for i in range(nb):
        for lg in range(n_lane_groups):
            xp_ref[lg, pl.ds(i, SUBLANES, stride=pitch), :] = (
                x_ref[i * SUBLANES:(i + 1) * SUBLANES, lg * LANES:(lg + 1) * LANES])
    xp = jnp.concatenate(
        [jnp.concatenate([xp_ref[lg, r * pitch:r * pitch + nb, :] for r in range(SUBLANES)], axis=0)
         for lg in range(n_lane_groups)], axis=1)
    h_ref[...] = _rms(xp, norm_ref[...]).astype(BF16)

    first_row = lax.broadcasted_iota(jnp.int32, (SUBLANES, FF_CHUNK), 0) == 0

    def shift_stream(blk, carried):
        rolled = pltpu.roll(blk, 1, 0)
        head = jnp.where(first_row, carried, rolled[0:SUBLANES])
        return jnp.concatenate([head, rolled[SUBLANES:]], axis=0), rolled[0:SUBLANES]

    def conv(a, cols):
        s6, c6 = shift_stream(a[6 * nb:7 * nb], carry_ref[0, :, cols])
        s7, c7 = shift_stream(a[7 * nb:8 * nb], carry_ref[1, :, cols])
        carry_ref[0, :, cols] = c6
        carry_ref[1, :, cols] = c7
        back1 = jnp.concatenate([s7, a[0:7 * nb]], axis=0)
        back2 = jnp.concatenate([s6, s7, a[0:6 * nb]], axis=0)
        return (conv_b_ref[:, cols] + back2 * conv_ref[0:1, cols]
                + back1 * conv_ref[1:2, cols] + a * conv_ref[2:3, cols])

    halves = [None, None]
    for j in range(N_FF_CHUNKS):
        for part in range(2):
            c0 = part * D_FF + j * FF_CHUNK
            cols = slice(c0, c0 + FF_CHUNK)
            halves[part] = conv(_dot(h_ref[...], w_up_ref[:, cols]), cols)
        g_ref[:, j * FF_CHUNK:(j + 1) * FF_CHUNK] = (
            _gelu_tanh(halves[0]) * halves[1]).astype(BF16)
    groups_per_piece = PIECE_COLS // LANES
    for p in range(D_MODEL // PIECE_COLS):
        cols = slice(p * PIECE_COLS, (p + 1) * PIECE_COLS)
        y = _dot(g_ref[...], w_down_ref[:, cols])
        for r in range(SUBLANES):
            rows = slice(r * nb, (r + 1) * nb)
            for k in range(groups_per_piece):
                op_ref[p * groups_per_piece + k, pl.ds(r, nb, stride=SUBLANES), :] = (
                    y[rows, k * LANES:(k + 1) * LANES])
        y_nat = jnp.concatenate(
            [op_ref[p * groups_per_piece + k] for k in range(groups_per_piece)], axis=1)
        out_ref[:, cols] = x_ref[:, cols] + y_nat


def _ffn(x2d, norm, w_up, conv, conv_b, w_down, tiles_per_batch):
    tokens = x2d.shape[0]
    row = pl.BlockSpec((FFN_TILE, D_MODEL), lambda i: (i, 0))
    return pl.pallas_call(
        functools.partial(_ffn_kernel, tiles_per_batch),
        out_shape=jax.ShapeDtypeStruct((tokens, D_MODEL), F32),
        grid=(tokens // FFN_TILE,),
        in_specs=[row, _const_spec((1, D_MODEL)),
                  _const_spec((D_MODEL, 2 * D_FF)),
                  _const_spec((CONV_W, 2 * D_FF)),
                  _const_spec((1, 2 * D_FF)),
                  _const_spec((D_FF, D_MODEL))],
        out_specs=row,
        scratch_shapes=[
            pltpu.VMEM((D_MODEL // LANES, FFN_TILE + SUBLANES * SUBLANES, LANES), F32),
            pltpu.VMEM((D_MODEL // LANES, FFN_TILE, LANES), F32),
            pltpu.VMEM((FFN_TILE, D_MODEL), BF16),
            pltpu.VMEM((2, SUBLANES, 2 * D_FF), F32),
            pltpu.VMEM((FFN_TILE, D_FF), BF16),
        ],
        compiler_params=pltpu.CompilerParams(
            dimension_semantics=("arbitrary",), vmem_limit_bytes=VMEM_LIMIT),
        name="conv_ffn",
    )(x2d, norm, w_up, conv, conv_b, w_down)


def kernel(x, mem, positions, mix_norm, w_in, q_norm, k_norm, attn_sinks, gmlp_v_norm, gmlp_ws, gmlp_bs, attn_out_norm, gmlp_out_norm, w_out, xa_norm, mem_norm, xa_wq, xa_wkv, xa_q_norm, xa_k_norm, xa_wo, ffn_norm, ffn_up, ffn_conv, ffn_conv_b, ffn_down):
    batch, seq, d_model = x.shape
    depth = mix_norm.shape[0]
    assert d_model == D_MODEL and mem.shape[1] == MEM_LEN
    assert seq % MIX_TILE == 0 and seq % FFN_TILE == 0 and seq % TILE_M == 0
    tokens = batch * seq

    inv_freq = 1.0 / (ROPE_THETA ** (jnp.arange(HALF, dtype=F32) * (2.0 / HEAD_DIM)))
    cos, sin = _rope_tables(positions, inv_freq)
    reps = LANES // HEAD_DIM

    x2d = x.reshape(tokens, D_MODEL)
    mem2d = mem.reshape(batch * MEM_LEN, D_MODEL)
    row = lambda a: a.reshape(1, -1)
    for l in range(depth):
        gbias = jnp.repeat(gmlp_bs[l].T, HEAD_DIM, axis=1)
        x2d = _mixer(x2d, cos, sin, attn_sinks[l], row(mix_norm[l]), w_in[l].astype(BF16),
                     row(jnp.tile(q_norm[l], reps)), row(jnp.tile(k_norm[l], reps)),
                     row(gmlp_v_norm[l]), gmlp_ws[l], gbias,
                     row(attn_out_norm[l]), row(gmlp_out_norm[l]), w_out[l].astype(BF16),
                     seq // MIX_TILE)
        k_mem, v_mem = _mem_kv(mem2d, row(mem_norm[l]), xa_wkv[l].astype(BF16),
                               row(xa_k_norm[l]), batch)
        x2d = _xattn(x2d, row(xa_norm[l]), xa_wq[l].astype(BF16), row(xa_q_norm[l]),
                     k_mem, v_mem, xa_wo[l].astype(BF16), seq // TILE_M)
        x2d = _ffn(x2d, row(ffn_norm[l]), ffn_up[l].astype(BF16), ffn_conv[l],
                   row(ffn_conv_b[l]), ffn_down[l].astype(BF16), seq // FFN_TILE)
    return x2d.reshape(batch, seq, D_MODEL)
```

```python
import functools
import math

import jax
import jax.numpy as jnp
from jax import lax
from jax.experimental import pallas as pl
from jax.experimental.pallas import tpu as pltpu

F32 = jnp.float32
BF16 = jnp.bfloat16

LANES = 128
SUBLANES = 8

D_MODEL = 1024
HEAD_DIM = 64
HALF = HEAD_DIM // 2
ATTN_W = 512
KV_W = 128
GM_W = 512
IN_COLS = ATTN_W + 2 * KV_W + 2 * GM_W
BLK = 128
ROPE_THETA = 10000.0
XA_HEADS = 4
XA_DH = 256
MEM_LEN = 256
D_FF = 2816
FF_CHUNK = 256
N_FF_CHUNKS = D_FF // FF_CHUNK
ROPE_ROWS = 8
PIECE_COLS = 256
GZ0 = ATTN_W + 2 * KV_W
CONV_W = 3
EPS = 1e-6
MASK_BIAS = -1e30

SUB_M = 512
MIX_TILE = SUB_M
FFN_TILE = 2 * SUB_M
TILE_M = 2 * SUB_M
VMEM_LIMIT = 56 * 1024 * 1024


def _rms(x, g):
    return x * lax.rsqrt(jnp.mean(x * x, axis=-1, keepdims=True) + EPS) * g


def _gelu_tanh(x):
    c = math.sqrt(2.0 / math.pi)
    return x * (0.5 * (1.0 + jnp.tanh(c * (x + 0.044715 * (x * x * x)))))


def _dot(a, b):
    return jnp.dot(a, b, preferred_element_type=F32)


def _dot_nt(a, b):
    return lax.dot_general(a, b, (((1,), (1,)), ((), ())), preferred_element_type=F32)


def _const_spec(shape):
    nd = len(shape)
    return pl.BlockSpec(shape, lambda i: (0,) * nd, pipeline_mode=pl.Buffered(1))


def _rope_kernel(inv_freq_ref, pos_ref, cos_ref, sin_ref):
    inv_freq = inv_freq_ref[...]
    for g in range(ROPE_ROWS):
        pos = pos_ref[g:g + 1, :].astype(F32)
        ang = pos * inv_freq
        c = jnp.cos(ang)
        s = jnp.sin(ang)
        reps = LANES // HEAD_DIM
        c_all = jnp.concatenate([c, c] * reps, axis=0)
        s_all = jnp.concatenate([-s, s] * reps, axis=0)
        rows = slice(g * LANES, (g + 1) * LANES)
        cos_ref[rows, :] = c_all.T
        sin_ref[rows, :] = s_all.T


def _rope_tables(positions, inv_freq):
    tokens = positions.size
    pos2d = positions.reshape(tokens // LANES, LANES)
    out = jax.ShapeDtypeStruct((tokens, LANES), F32)
    out_spec = pl.BlockSpec((ROPE_ROWS * LANES, LANES), lambda i: (i, 0))
    return pl.pallas_call(
        _rope_kernel,
        out_shape=(out, out),
        grid=(tokens // (ROPE_ROWS * LANES),),
        in_specs=[_const_spec((HALF, 1)),
                  pl.BlockSpec((ROPE_ROWS, LANES), lambda i: (i, 0))],
        out_specs=(out_spec, out_spec),
        compiler_params=pltpu.CompilerParams(dimension_semantics=("arbitrary",)),
        name="rope_tables",
    )(inv_freq.reshape(HALF, 1), pos2d)


def _mem_kv_kernel(mem_ref, mem_norm_ref, wkv_ref, k_norm_ref, k_ref, v_ref):
    m = _rms(mem_ref[...], mem_norm_ref[...]).astype(BF16)
    kv = _dot(m, wkv_ref[...])
    scale = 1.0 / math.sqrt(XA_DH)
    for h in range(XA_HEADS):
        cols = slice(h * XA_DH, (h + 1) * XA_DH)
        k_ref[:, cols] = (_rms(kv[:, cols], k_norm_ref[...]) * scale).astype(BF16)
    v_ref[...] = kv[:, D_MODEL:].astype(BF16)


def _mem_kv(mem2d, mem_norm, wkv, k_norm, batch):
    out = jax.ShapeDtypeStruct((batch * MEM_LEN, D_MODEL), BF16)
    blk = pl.BlockSpec((MEM_LEN, D_MODEL), lambda b: (b, 0))
    return pl.pallas_call(
        _mem_kv_kernel,
        out_shape=(out, out),
        grid=(batch,),
        in_specs=[blk, _const_spec((1, D_MODEL)), _const_spec((D_MODEL, 2 * D_MODEL)),
                  _const_spec((1, XA_DH))],
        out_specs=(blk, blk),
        compiler_params=pltpu.CompilerParams(
            dimension_semantics=("arbitrary",), vmem_limit_bytes=VMEM_LIMIT),
        name="mem_kv",
    )(mem2d, mem_norm, wkv, k_norm)


def _mixer_kernel(tiles_per_batch,
                  sinks_ref, xa_ref, xc_ref, cos_ref, sin_ref, mix_norm_ref, w_in_ref,
                  q_norm_ref, k_norm_ref, gv_norm_ref, ws_ref, gbias_ref,
                  attn_norm_ref, gm_norm_ref, w_out_ref,
                  out_ref,
                  proj2_ref, hy_ref,
                  kbuf_ref, vbuf_ref, wcat_ref, bias_ref):
    step = pl.program_id(0)
    first_tile = ((step + tiles_per_batch - 1) % tiles_per_batch) == 0
    n_blocks = MIX_TILE // BLK
    proj_refs = (proj2_ref.at[0], proj2_ref.at[1])
    h_plane, y_plane = 0, (1, 2)
    y_refs = (hy_ref.at[y_plane[0]], hy_ref.at[y_plane[1]])

    @pl.when(step == 0)
    def _build_constants():
        t = lax.broadcasted_iota(jnp.int32, (BLK, BLK), 0)
        s = lax.broadcasted_iota(jnp.int32, (BLK, BLK), 1)
        for j in range(GM_W // LANES):
            w_a = jnp.where(s <= t, ws_ref[2 * j], 0.0)
            w_b = jnp.where(s <= t, ws_ref[2 * j + 1], 0.0)
            wcat_ref[j] = jnp.concatenate([w_a, w_b], axis=1).astype(BF16)
        qi = lax.broadcasted_iota(jnp.int32, (2 * BLK, 4 * BLK), 0) & (BLK - 1)
        kj = lax.broadcasted_iota(jnp.int32, (2 * BLK, 4 * BLK), 1) & (2 * BLK - 1)
        cur = (kj >= BLK) & ((kj - BLK) <= qi)
        prev = (kj < BLK) & (kj > qi)
        bias_ref[0] = jnp.where(cur | prev, 0.0, MASK_BIAS)
        bias_ref[1] = jnp.where(cur, 0.0, MASK_BIAS)
        proj2_ref[1] = jnp.zeros((MIX_TILE, IN_COLS), F32)
        hy_ref[y_plane[1]] = jnp.zeros((MIX_TILE, D_MODEL), BF16)

    @pl.when(first_tile)
    def _reset_carry():
        kbuf_ref[0:BLK, :] = jnp.zeros((BLK, KV_W), F32)
        vbuf_ref[0:BLK, :] = jnp.zeros((BLK, KV_W), F32)

    lane = lax.broadcasted_iota(jnp.int32, (BLK, LANES), 1)
    low_head = lane < HEAD_DIM
    first_half = (lane & (HEAD_DIM - 1)) < HALF
    r = lax.broadcasted_iota(jnp.int32, (LANES, LANES), 0)
    c = lax.broadcasted_iota(jnp.int32, (LANES, LANES), 1)
    head_ones = jnp.where((r < HEAD_DIM) == (c < HEAD_DIM), 1.0, 0.0).astype(BF16)
    lane2 = lax.broadcasted_iota(jnp.int32, (2 * BLK, LANES), 1)
    low_head2 = lane2 < HEAD_DIM

    def head_norm_rope(t, gain, cos, sin):
        ss = _dot((t * t).astype(BF16), head_ones)
        tn = t * lax.rsqrt(ss * (1.0 / HEAD_DIM) + EPS) * gain
        rot = jnp.where(first_half, pltpu.roll(tn, LANES - HALF, 1), pltpu.roll(tn, HALF, 1))
        return tn * cos + rot * sin

    def block(b, proj_ref, y_ref):
        r0 = b * BLK
        rows = pl.ds(r0, BLK)
        cos = cos_ref[rows, :]
        sin = sin_ref[rows, :]

        k = head_norm_rope(proj_ref[rows, ATTN_W:ATTN_W + KV_W], k_norm_ref[...], cos, sin)
        kbuf_ref[pl.ds(r0 + BLK, BLK), :] = k
        vbuf_ref[pl.ds(r0 + BLK, BLK), :] = proj_ref[rows, ATTN_W + KV_W:ATTN_W + 2 * KV_W]
        kcat = kbuf_ref[pl.ds(r0, 2 * BLK), :]
        vcat = vbuf_ref[pl.ds(r0, 2 * BLK), :]
        kswp = pltpu.roll(kcat, HEAD_DIM, 1)
        vswp = pltpu.roll(vcat, HEAD_DIM, 1)

        bias = bias_ref[first_tile.astype(jnp.int32)] if b == 0 else bias_ref[0]

        attn_parts = []
        for g in range(2):
            k_lo, k_hi = (kcat, kswp) if g == 0 else (kswp, kcat)
            v_lo, v_hi = (vcat, vswp) if g == 0 else (vswp, vcat)
            k_big = jnp.concatenate([jnp.where(low_head2, k_lo, 0.0),
                                     jnp.where(low_head2, 0.0, k_hi)], axis=0).astype(BF16)
            v_big = jnp.concatenate([jnp.where(low_head2, v_lo, 0.0),
                                     jnp.where(low_head2, 0.0, v_hi)], axis=0).astype(BF16)
            q_pairs = []
            for pair in (2 * g, 2 * g + 1):
                qcols = slice(pair * LANES, (pair + 1) * LANES)
                q = head_norm_rope(proj_ref[rows, qcols], q_norm_ref[...], cos, sin)
                q_pairs.append((q * (1.0 / math.sqrt(HEAD_DIM))).astype(BF16))
            qs = jnp.concatenate(q_pairs, axis=0)
            s = _dot_nt(qs, k_big) + bias
            yield
            p_rows, inv_rows = [], []
            for rg in range(2):
                p_cols, inv_cols = [], []
                for half in range(2):
                    sink = sinks_ref[4 * g + 2 * rg + half]
                    sh = s[rg * BLK:(rg + 1) * BLK, half * 2 * BLK:(half + 1) * 2 * BLK]
                    m = jnp.maximum(jnp.max(sh, axis=-1, keepdims=True), sink)
                    p = jnp.exp(sh - m)
                    denom = jnp.sum(p, axis=-1, keepdims=True) + jnp.exp(sink - m)
                    p_cols.append(p.astype(BF16))
                    inv_cols.append(1.0 / denom)
                p_rows.append(jnp.concatenate(p_cols, axis=1))
                inv_rows.append(jnp.where(low_head, inv_cols[0], inv_cols[1]))
            o = _dot(jnp.concatenate(p_rows, axis=0), v_big)
            attn_parts.append(o[0:BLK] * inv_rows[0])
            attn_parts.append(o[BLK:2 * BLK] * inv_rows[1])
        attn = jnp.concatenate(attn_parts, axis=1)
        y_ref[rows, 0:ATTN_W] = _rms(attn, attn_norm_ref[...]).astype(BF16)
        yield

        gu = _gelu_tanh(proj_ref[rows, GZ0:GZ0 + GM_W])
        gv = _rms(_gelu_tanh(proj_ref[rows, GZ0 + GM_W:GZ0 + 2 * GM_W]), gv_norm_ref[...])
        mixed = []
        for j in range(GM_W // LANES):
            vb = gv[:, j * LANES:(j + 1) * LANES]
            v_bd = jnp.concatenate([jnp.where(low_head, vb, 0.0),
                                    jnp.where(low_head, 0.0, vb)], axis=0).astype(BF16)
            mixed.append(_dot(wcat_ref[j], v_bd))
        gm = gu * (jnp.concatenate(mixed, axis=1) + gbias_ref[...])
        y_ref[rows, ATTN_W:ATTN_W + GM_W] = _rms(gm, gm_norm_ref[...]).astype(BF16)

    def body(cur, prev):
        def in_piece(p):
            cols = slice(p * PIECE_COLS, (p + 1) * PIECE_COLS)
            proj_refs[cur][:, cols] = _dot(hy_ref[h_plane], w_in_ref[:, cols])

        def out_piece(p):
            cols = slice(p * PIECE_COLS, (p + 1) * PIECE_COLS)
            out_ref[:, cols] = xc_ref[:, cols] + _dot(hy_ref[y_plane[prev]], w_out_ref[:, cols])

        hy_ref[h_plane] = _rms(xa_ref[...], mix_norm_ref[...]).astype(BF16)
        pieces = ([functools.partial(out_piece, p) for p in range(D_MODEL // PIECE_COLS)]
                  + [functools.partial(in_piece, p) for p in range(IN_COLS // PIECE_COLS)])
        pieces.reverse()
        for b in range(n_blocks):
            if pieces:
                pieces.pop()()
            for _ in block(b, proj_refs[prev], y_refs[cur]):
                if pieces:
                    pieces.pop()()
        while pieces:
            pieces.pop()()

    for parity in range(2):
        pl.when((step % 2) == parity)(functools.partial(body, parity, 1 - parity))

    kbuf_ref[0:BLK, :] = kbuf_ref[MIX_TILE:MIX_TILE + BLK, :]
    vbuf_ref[0:BLK, :] = vbuf_ref[MIX_TILE:MIX_TILE + BLK, :]


def _mixer(x2d, cos, sin, sinks, mix_norm, w_in, q_norm, k_norm, gv_norm, ws, gbias,
           attn_norm, gm_norm, w_out, tiles_per_batch):
    tokens = x2d.shape[0]
    n_tiles = tokens // MIX_TILE

    def stage(lag, w):
        return pl.BlockSpec((MIX_TILE, w), lambda i: (jnp.clip(i - lag, 0, n_tiles - 1), 0))

    return pl.pallas_call(
        functools.partial(_mixer_kernel, tiles_per_batch),
        out_shape=jax.ShapeDtypeStruct((tokens, D_MODEL), F32),
        grid=(n_tiles + 2,),
        in_specs=[pl.BlockSpec(memory_space=pltpu.SMEM),
                  stage(0, D_MODEL), stage(2, D_MODEL), stage(1, LANES), stage(1, LANES),
                  _const_spec((1, D_MODEL)), _const_spec((D_MODEL, IN_COLS)),
                  _const_spec((1, LANES)), _const_spec((1, LANES)),
                  _const_spec((1, GM_W)), _const_spec((GM_W // HEAD_DIM, BLK, BLK)),
                  _const_spec((BLK, GM_W)),
                  _const_spec((1, ATTN_W)), _const_spec((1, GM_W)),
                  _const_spec((D_MODEL, D_MODEL))],
        out_specs=stage(2, D_MODEL),
        scratch_shapes=[
            pltpu.VMEM((2, MIX_TILE, IN_COLS), F32),
            pltpu.VMEM((3, MIX_TILE, D_MODEL), BF16),
            pltpu.VMEM((MIX_TILE + BLK, KV_W), F32),
            pltpu.VMEM((MIX_TILE + BLK, KV_W), F32),
            pltpu.VMEM((GM_W // LANES, BLK, 2 * BLK), BF16),
            pltpu.VMEM((2, 2 * BLK, 4 * BLK), F32),
        ],
        compiler_params=pltpu.CompilerParams(
            dimension_semantics=("arbitrary",), vmem_limit_bytes=VMEM_LIMIT),
        name="mixer",
    )(sinks, x2d, x2d, cos, sin, mix_norm, w_in, q_norm, k_norm,
      gv_norm, ws, gbias, attn_norm, gm_norm, w_out)


def _xattn_kernel(x_ref, xa_norm_ref, wq_ref, q_norm_ref, k_ref, v_ref, wo_ref, out_ref):
    x = x_ref[...]
    h = _rms(x, xa_norm_ref[...]).astype(BF16)
    q = _dot(h, wq_ref[...])
    outs = []
    for hd in range(XA_HEADS):
        cols = slice(hd * XA_DH, (hd + 1) * XA_DH)
        qh = _rms(q[:, cols], q_norm_ref[...]).astype(BF16)
        s = _dot_nt(qh, k_ref[:, cols])
        p = jnp.exp(s - jnp.max(s, axis=-1, keepdims=True))
        inv = 1.0 / jnp.sum(p, axis=-1, keepdims=True)
        outs.append((_dot(p.astype(BF16), v_ref[:, cols]) * inv).astype(BF16))
    out_ref[...] = x + _dot(jnp.concatenate(outs, axis=1), wo_ref[...])


def _xattn(x2d, xa_norm, wq, q_norm, k, v, wo, tiles_per_batch):
    tokens = x2d.shape[0]
    row = pl.BlockSpec((TILE_M, D_MODEL), lambda i: (i, 0))
    kv = pl.BlockSpec((MEM_LEN, D_MODEL), lambda i: (i // tiles_per_batch, 0))
    return pl.pallas_call(
        _xattn_kernel,
        out_shape=jax.ShapeDtypeStruct((tokens, D_MODEL), F32),
        grid=(tokens // TILE_M,),
        in_specs=[row, _const_spec((1, D_MODEL)), _const_spec((D_MODEL, D_MODEL)),
                  _const_spec((1, XA_DH)), kv, kv, _const_spec((D_MODEL, D_MODEL))],
        out_specs=row,
        compiler_params=pltpu.CompilerParams(
            dimension_semantics=("arbitrary",), vmem_limit_bytes=VMEM_LIMIT),
        name="xattn",
    )(x2d, xa_norm, wq, q_norm, k, v, wo)


def _ffn_kernel(tiles_per_batch,
                x_ref, norm_ref, w_up_ref, conv_ref, conv_b_ref, w_down_ref,
                out_ref,
                xp_ref, op_ref, h_ref, carry_ref, g_ref):
    step = pl.program_id(0)
    nb = FFN_TILE // SUBLANES

    @pl.when((step % tiles_per_batch) == 0)
    def _reset_carry():
        carry_ref[...] = jnp.zeros(carry_ref.shape, F32)

    n_lane_groups = D_MODEL // LANES
    pitch = nb + SUBLANES
    for i in range(nb):
        for lg in range(n_lane_groups):
            xp_ref[lg, pl.ds(i, SUBLANES, stride=pitch), :] = (
                x_ref[i * SUBLANES:(i + 1) * SUBLANES, lg * LANES:(lg + 1) * LANES])
    xp = jnp.concatenate(
        [jnp.concatenate([xp_ref[lg, r * pitch:r * pitch + nb, :] for r in range(SUBLANES)], axis=0)
         for lg in range(n_lane_groups)], axis=1)
    h_ref[...] = _rms(xp, norm_ref[...]).astype(BF16)

    first_row = lax.broadcasted_iota(jnp.int32, (SUBLANES, FF_CHUNK), 0) == 0

    def shift_stream(blk, carried):
        rolled = pltpu.roll(blk, 1, 0)
        head = jnp.where(first_row, carried, rolled[0:SUBLANES])
        return jnp.concatenate([head, rolled[SUBLANES:]], axis=0), rolled[0:SUBLANES]

    def conv(a, cols):
        s6, c6 = shift_stream(a[6 * nb:7 * nb], carry_ref[0, :, cols])
        s7, c7 = shift_stream(a[7 * nb:8 * nb], carry_ref[1, :, cols])
        carry_ref[0, :, cols] = c6
        carry_ref[1, :, cols] = c7
        back1 = jnp.concatenate([s7, a[0:7 * nb]], axis=0)
        back2 = jnp.concatenate([s6, s7, a[0:6 * nb]], axis=0)
        return (conv_b_ref[:, cols] + back2 * conv_ref[0:1, cols]
                + back1 * conv_ref[1:2, cols] + a * conv_ref[2:3, cols])

    halves = [None, None]
    for j in range(N_FF_CHUNKS):
        for part in range(2):
            c0 = part * D_FF + j * FF_CHUNK
            cols = slice(c0, c0 + FF_CHUNK)
            halves[part] = conv(_dot(h_ref[...], w_up_ref[:, cols]), cols)
        g_ref[:, j * FF_CHUNK:(j + 1) * FF_CHUNK] = (
            _gelu_tanh(halves[0]) * halves[1]).astype(BF16)
    groups_per_piece = PIECE_COLS // LANES
    for p in range(D_MODEL // PIECE_COLS):
        cols = slice(p * PIECE_COLS, (p + 1) * PIECE_COLS)
        y = _dot(g_ref[...], w_down_ref[:, cols])
        for r in range(SUBLANES):
            rows = slice(r * nb, (r + 1) * nb)
            for k in range(groups_per_piece):
                op_ref[p * groups_per_piece + k, pl.ds(r, nb, stride=SUBLANES), :] = (
                    y[rows, k * LANES:(k + 1) * LANES])
        y_nat = jnp.concatenate(
            [op_ref[p * groups_per_piece + k] for k in range(groups_per_piece)], axis=1)
        out_ref[:, cols] = x_ref[:, cols] + y_nat


def _ffn(x2d, norm, w_up, conv, conv_b, w_down, tiles_per_batch):
    tokens = x2d.shape[0]
    row = pl.BlockSpec((FFN_TILE, D_MODEL), lambda i: (i, 0))
    return pl.pallas_call(
        functools.partial(_ffn_kernel, tiles_per_batch),
        out_shape=jax.ShapeDtypeStruct((tokens, D_MODEL), F32),
        grid=(tokens // FFN_TILE,),
        in_specs=[row, _const_spec((1, D_MODEL)),
                  _const_spec((D_MODEL, 2 * D_FF)),
                  _const_spec((CONV_W, 2 * D_FF)),
                  _const_spec((1, 2 * D_FF)),
                  _const_spec((D_FF, D_MODEL))],
        out_specs=row,
        scratch_shapes=[
            pltpu.VMEM((D_MODEL // LANES, FFN_TILE + SUBLANES * SUBLANES, LANES), F32),
            pltpu.VMEM((D_MODEL // LANES, FFN_TILE, LANES), F32),
            pltpu.VMEM((FFN_TILE, D_MODEL), BF16),
            pltpu.VMEM((2, SUBLANES, 2 * D_FF), F32),
            pltpu.VMEM((FFN_TILE, D_FF), BF16),
        ],
        compiler_params=pltpu.CompilerParams(
            dimension_semantics=("arbitrary",), vmem_limit_bytes=VMEM_LIMIT),
        name="conv_ffn",
    )(x2d, norm, w_up, conv, conv_b, w_down)


def kernel(x, mem, positions, mix_norm, w_in, q_norm, k_norm, attn_sinks, gmlp_v_norm, gmlp_ws, gmlp_bs, attn_out_norm, gmlp_out_norm, w_out, xa_norm, mem_norm, xa_wq, xa_wkv, xa_q_norm, xa_k_norm, xa_wo, ffn_norm, ffn_up, ffn_conv, ffn_conv_b, ffn_down):
    batch, seq, d_model = x.shape
    depth = mix_norm.shape[0]
    assert d_model == D_MODEL and mem.shape[1] == MEM_LEN
    assert seq % MIX_TILE == 0 and seq % FFN_TILE == 0 and seq % TILE_M == 0
    tokens = batch * seq

    inv_freq = 1.0 / (ROPE_THETA ** (jnp.arange(HALF, dtype=F32) * (2.0 / HEAD_DIM)))
    cos, sin = _rope_tables(positions, inv_freq)
    reps = LANES // HEAD_DIM

    x2d = x.reshape(tokens, D_MODEL)
    mem2d = mem.reshape(batch * MEM_LEN, D_MODEL)
    row = lambda a: a.reshape(1, -1)
    for l in range(depth):
        gbias = jnp.repeat(gmlp_bs[l].T, HEAD_DIM, axis=1)
        x2d = _mixer(x2d, cos, sin, attn_sinks[l], row(mix_norm[l]), w_in[l].astype(BF16),
                     row(jnp.tile(q_norm[l], reps)), row(jnp.tile(k_norm[l], reps)),
                     row(gmlp_v_norm[l]), gmlp_ws[l], gbias,
                     row(attn_out_norm[l]), row(gmlp_out_norm[l]), w_out[l].astype(BF16),
                     seq // MIX_TILE)
        k_mem, v_mem = _mem_kv(mem2d, row(mem_norm[l]), xa_wkv[l].astype(BF16),
                               row(xa_k_norm[l]), batch)
        x2d = _xattn(x2d, row(xa_norm[l]), xa_wq[l].astype(BF16), row(xa_q_norm[l]),
                     k_mem, v_mem, xa_wo[l].astype(BF16), seq // TILE_M)
        x2d = _ffn(x2d, row(ffn_norm[l]), ffn_up[l].astype(BF16), ffn_conv[l],
                   row(ffn_conv_b[l]), ffn_down[l].astype(BF16), seq // FFN_TILE)
    return x2d.reshape(batch, seq, D_MODEL)
```

```python
import functools
import math

import jax
import jax.numpy as jnp
from jax import lax
from jax.experimental import pallas as pl
from jax.experimental.pallas import tpu as pltpu

F32 = jnp.float32
BF16 = jnp.bfloat16

LANES = 128
SUBLANES = 8

D_MODEL = 1024
HEAD_DIM = 64
HALF = HEAD_DIM // 2
ATTN_W = 512
KV_W = 128
GM_W = 512
IN_COLS = ATTN_W + 2 * KV_W + 2 * GM_W
BLK = 128
ROPE_THETA = 10000.0
XA_HEADS = 4
XA_DH = 256
MEM_LEN = 256
D_FF = 2816
FF_CHUNK = 256
N_FF_CHUNKS = D_FF // FF_CHUNK
ROPE_ROWS = 8
PIECE_COLS = 256
GZ0 = ATTN_W + 2 * KV_W
CONV_W = 3
EPS = 1e-6
MASK_BIAS = -1e30

SUB_M = 512
MIX_TILE = SUB_M
FFN_TILE = 2 * SUB_M
TILE_M = 2 * SUB_M
VMEM_LIMIT = 56 * 1024 * 1024


def _rms(x, g):
    return x * lax.rsqrt(jnp.mean(x * x, axis=-1, keepdims=True) + EPS) * g


def _gelu_tanh(x):
    c = math.sqrt(2.0 / math.pi)
    half_x = 0.5 * x
    return half_x + half_x * jnp.tanh(x * (c + (c * 0.044715) * (x * x)))


def _dot(a, b):
    return jnp.dot(a, b, preferred_element_type=F32)


def _dot_nt(a, b):
    return lax.dot_general(a, b, (((1,), (1,)), ((), ())), preferred_element_type=F32)


def _const_spec(shape):
    nd = len(shape)
    return pl.BlockSpec(shape, lambda i: (0,) * nd, pipeline_mode=pl.Buffered(1))


def _rope_kernel(inv_freq_ref, pos_ref, cos_ref, sin_ref):
    inv_freq = inv_freq_ref[...]
    for g in range(ROPE_ROWS):
        pos = pos_ref[g:g + 1, :].astype(F32)
        ang = pos * inv_freq
        c = jnp.cos(ang)
        s = jnp.sin(ang)
        reps = LANES // HEAD_DIM
        c_all = jnp.concatenate([c, c] * reps, axis=0)
        s_all = jnp.concatenate([-s, s] * reps, axis=0)
        rows = slice(g * LANES, (g + 1) * LANES)
        cos_ref[rows, :] = c_all.T
        sin_ref[rows, :] = s_all.T


def _rope_tables(positions, inv_freq):
    tokens = positions.size
    pos2d = positions.reshape(tokens // LANES, LANES)
    out = jax.ShapeDtypeStruct((tokens, LANES), F32)
    out_spec = pl.BlockSpec((ROPE_ROWS * LANES, LANES), lambda i: (i, 0))
    return pl.pallas_call(
        _rope_kernel,
        out_shape=(out, out),
        grid=(tokens // (ROPE_ROWS * LANES),),
        in_specs=[_const_spec((HALF, 1)),
                  pl.BlockSpec((ROPE_ROWS, LANES), lambda i: (i, 0))],
        out_specs=(out_spec, out_spec),
        compiler_params=pltpu.CompilerParams(dimension_semantics=("arbitrary",)),
        name="rope_tables",
    )(inv_freq.reshape(HALF, 1), pos2d)


def _mem_kv_kernel(mem_ref, mem_norm_ref, wkv_ref, k_norm_ref, k_ref, v_ref):
    m = _rms(mem_ref[...], mem_norm_ref[...]).astype(BF16)
    kv = _dot(m, wkv_ref[...])
    scale = 1.0 / math.sqrt(XA_DH)
    for h in range(XA_HEADS):
        cols = slice(h * XA_DH, (h + 1) * XA_DH)
        k_ref[:, cols] = (_rms(kv[:, cols], k_norm_ref[...]) * scale).astype(BF16)
    v_ref[...] = kv[:, D_MODEL:].astype(BF16)


def _mem_kv(mem2d, mem_norm, wkv, k_norm, batch):
    out = jax.ShapeDtypeStruct((batch * MEM_LEN, D_MODEL), BF16)
    blk = pl.BlockSpec((MEM_LEN, D_MODEL), lambda b: (b, 0))
    return pl.pallas_call(
        _mem_kv_kernel,
        out_shape=(out, out),
        grid=(batch,),
        in_specs=[blk, _const_spec((1, D_MODEL)), _const_spec((D_MODEL, 2 * D_MODEL)),
                  _const_spec((1, XA_DH))],
        out_specs=(blk, blk),
        compiler_params=pltpu.CompilerParams(
            dimension_semantics=("arbitrary",), vmem_limit_bytes=VMEM_LIMIT),
        name="mem_kv",
    )(mem2d, mem_norm, wkv, k_norm)


def _mixer_kernel(tiles_per_batch,
                  sinks_ref, xa_ref, xc_ref, cos_ref, sin_ref, mix_norm_ref, w_in_ref,
                  q_norm_ref, k_norm_ref, gv_norm_ref, ws_ref, gbias_ref,
                  attn_norm_ref, gm_norm_ref, w_out_ref,
                  out_ref,
                  proj2_ref, hy_ref,
                  kbuf_ref, vbuf_ref, wcat_ref, bias_ref):
    step = pl.program_id(0)
    first_tile = ((step + tiles_per_batch - 1) % tiles_per_batch) == 0
    n_blocks = MIX_TILE // BLK
    proj_refs = (proj2_ref.at[0], proj2_ref.at[1])
    h_plane, y_plane = 0, (1, 2)
    y_refs = (hy_ref.at[y_plane[0]], hy_ref.at[y_plane[1]])

    @pl.when(step == 0)
    def _build_constants():
        t = lax.broadcasted_iota(jnp.int32, (BLK, BLK), 0)
        s = lax.broadcasted_iota(jnp.int32, (BLK, BLK), 1)
        for j in range(GM_W // LANES):
            w_a = jnp.where(s <= t, ws_ref[2 * j], 0.0)
            w_b = jnp.where(s <= t, ws_ref[2 * j + 1], 0.0)
            wcat_ref[j] = jnp.concatenate([w_a, w_b], axis=1).astype(BF16)
        qi = lax.broadcasted_iota(jnp.int32, (2 * BLK, 4 * BLK), 0) & (BLK - 1)
        kj = lax.broadcasted_iota(jnp.int32, (2 * BLK, 4 * BLK), 1) & (2 * BLK - 1)
        cur = (kj >= BLK) & ((kj - BLK) <= qi)
        prev = (kj < BLK) & (kj > qi)
        bias_ref[0] = jnp.where(cur | prev, 0.0, MASK_BIAS)
        bias_ref[1] = jnp.where(cur, 0.0, MASK_BIAS)
        proj2_ref[1] = jnp.zeros((MIX_TILE, IN_COLS), F32)
        hy_ref[y_plane[1]] = jnp.zeros((MIX_TILE, D_MODEL), BF16)

    @pl.when(first_tile)
    def _reset_carry():
        kbuf_ref[0:BLK, :] = jnp.zeros((BLK, KV_W), F32)
        vbuf_ref[0:BLK, :] = jnp.zeros((BLK, KV_W), F32)

    lane = lax.broadcasted_iota(jnp.int32, (BLK, LANES), 1)
    low_head = lane < HEAD_DIM
    first_half = (lane & (HEAD_DIM - 1)) < HALF
    r = lax.broadcasted_iota(jnp.int32, (LANES, LANES), 0)
    c = lax.broadcasted_iota(jnp.int32, (LANES, LANES), 1)
    head_ones = jnp.where((r < HEAD_DIM) == (c < HEAD_DIM), 1.0, 0.0).astype(BF16)
    lane2 = lax.broadcasted_iota(jnp.int32, (2 * BLK, LANES), 1)
    low_head2 = lane2 < HEAD_DIM

    def head_norm_rope(t, gain, cos, sin):
        ss = _dot((t * t).astype(BF16), head_ones)
        tn = t * lax.rsqrt(ss * (1.0 / HEAD_DIM) + EPS) * gain
        rot = jnp.where(first_half, pltpu.roll(tn, LANES - HALF, 1), pltpu.roll(tn, HALF, 1))
        return tn * cos + rot * sin

    def block(b, proj_ref, y_ref):
        r0 = b * BLK
        rows = pl.ds(r0, BLK)
        cos = cos_ref[rows, :]
        sin = sin_ref[rows, :]

        k = head_norm_rope(proj_ref[rows, ATTN_W:ATTN_W + KV_W], k_norm_ref[...], cos, sin)
        kbuf_ref[pl.ds(r0 + BLK, BLK), :] = k
        vbuf_ref[pl.ds(r0 + BLK, BLK), :] = proj_ref[rows, ATTN_W + KV_W:ATTN_W + 2 * KV_W]
        kcat = kbuf_ref[pl.ds(r0, 2 * BLK), :]
        vcat = vbuf_ref[pl.ds(r0, 2 * BLK), :]
        kswp = pltpu.roll(kcat, HEAD_DIM, 1)
        vswp = pltpu.roll(vcat, HEAD_DIM, 1)

        bias = bias_ref[first_tile.astype(jnp.int32)] if b == 0 else bias_ref[0]

        attn_parts = []
        for g in range(2):
            k_lo, k_hi = (kcat, kswp) if g == 0 else (kswp, kcat)
            v_lo, v_hi = (vcat, vswp) if g == 0 else (vswp, vcat)
            k_big = jnp.concatenate([jnp.where(low_head2, k_lo, 0.0),
                                     jnp.where(low_head2, 0.0, k_hi)], axis=0).astype(BF16)
            v_big = jnp.concatenate([jnp.where(low_head2, v_lo, 0.0),
                                     jnp.where(low_head2, 0.0, v_hi)], axis=0).astype(BF16)
            q_pairs = []
            for pair in (2 * g, 2 * g + 1):
                qcols = slice(pair * LANES, (pair + 1) * LANES)
                q = head_norm_rope(proj_ref[rows, qcols], q_norm_ref[...], cos, sin)
                q_pairs.append((q * (1.0 / math.sqrt(HEAD_DIM))).astype(BF16))
            qs = jnp.concatenate(q_pairs, axis=0)
            s = _dot_nt(qs, k_big) + bias
            yield
            p_rows, inv_rows = [], []
            for rg in range(2):
                p_cols, inv_cols = [], []
                for half in range(2):
                    sink = sinks_ref[4 * g + 2 * rg + half]
                    sh = s[rg * BLK:(rg + 1) * BLK, half * 2 * BLK:(half + 1) * 2 * BLK]
                    m = jnp.maximum(jnp.max(sh, axis=-1, keepdims=True), sink)
                    p = jnp.exp(sh - m)
                    denom = jnp.sum(p, axis=-1, keepdims=True) + jnp.exp(sink - m)
                    p_cols.append(p.astype(BF16))
                    inv_cols.append(1.0 / denom)
                p_rows.append(jnp.concatenate(p_cols, axis=1))
                inv_rows.append(jnp.where(low_head, inv_cols[0], inv_cols[1]))
            o = _dot(jnp.concatenate(p_rows, axis=0), v_big)
            attn_parts.append(o[0:BLK] * inv_rows[0])
            attn_parts.append(o[BLK:2 * BLK] * inv_rows[1])
        attn = jnp.concatenate(attn_parts, axis=1)
        y_ref[rows, 0:ATTN_W] = _rms(attn, attn_norm_ref[...]).astype(BF16)
        yield

        gu = _gelu_tanh(proj_ref[rows, GZ0:GZ0 + GM_W])
        gv = _rms(_gelu_tanh(proj_ref[rows, GZ0 + GM_W:GZ0 + 2 * GM_W]), gv_norm_ref[...])
        mixed = []
        for j in range(GM_W // LANES):
            vb = gv[:, j * LANES:(j + 1) * LANES]
            v_bd = jnp.concatenate([jnp.where(low_head, vb, 0.0),
                                    jnp.where(low_head, 0.0, vb)], axis=0).astype(BF16)
            mixed.append(_dot(wcat_ref[j], v_bd))
        gm = gu * (jnp.concatenate(mixed, axis=1) + gbias_ref[...])
        y_ref[rows, ATTN_W:ATTN_W + GM_W] = _rms(gm, gm_norm_ref[...]).astype(BF16)

    def body(cur, prev):
        def in_piece(p):
            cols = slice(p * PIECE_COLS, (p + 1) * PIECE_COLS)
            proj_refs[cur][:, cols] = _dot(hy_ref[h_plane], w_in_ref[:, cols])

        def out_piece(p):
            cols = slice(p * PIECE_COLS, (p + 1) * PIECE_COLS)
            out_ref[:, cols] = xc_ref[:, cols] + _dot(hy_ref[y_plane[prev]], w_out_ref[:, cols])

        hy_ref[h_plane] = _rms(xa_ref[...], mix_norm_ref[...]).astype(BF16)
        pieces = ([functools.partial(out_piece, p) for p in range(D_MODEL // PIECE_COLS)]
                  + [functools.partial(in_piece, p) for p in range(IN_COLS // PIECE_COLS)])
        pieces.reverse()
        for b in range(n_blocks):
            if pieces:
                pieces.pop()()
            for _ in block(b, proj_refs[prev], y_refs[cur]):
                if pieces:
                    pieces.pop()()
        while pieces:
            pieces.pop()()

    for parity in range(2):
        pl.when((step % 2) == parity)(functools.partial(body, parity, 1 - parity))

    kbuf_ref[0:BLK, :] = kbuf_ref[MIX_TILE:MIX_TILE + BLK, :]
    vbuf_ref[0:BLK, :] = vbuf_ref[MIX_TILE:MIX_TILE + BLK, :]


def _mixer(x2d, cos, sin, sinks, mix_norm, w_in, q_norm, k_norm, gv_norm, ws, gbias,
           attn_norm, gm_norm, w_out, tiles_per_batch):
    tokens = x2d.shape[0]
    n_tiles = tokens // MIX_TILE

    def stage(lag, w):
        return pl.BlockSpec((MIX_TILE, w), lambda i: (jnp.clip(i - lag, 0, n_tiles - 1), 0))

    return pl.pallas_call(
        functools.partial(_mixer_kernel, tiles_per_batch),
        out_shape=jax.ShapeDtypeStruct((tokens, D_MODEL), F32),
        grid=(n_tiles + 2,),
        in_specs=[pl.BlockSpec(memory_space=pltpu.SMEM),
                  stage(0, D_MODEL), stage(2, D_MODEL), stage(1, LANES), stage(1, LANES),
                  _const_spec((1, D_MODEL)), _const_spec((D_MODEL, IN_COLS)),
                  _const_spec((1, LANES)), _const_spec((1, LANES)),
                  _const_spec((1, GM_W)), _const_spec((GM_W // HEAD_DIM, BLK, BLK)),
                  _const_spec((BLK, GM_W)),
                  _const_spec((1, ATTN_W)), _const_spec((1, GM_W)),
                  _const_spec((D_MODEL, D_MODEL))],
        out_specs=stage(2, D_MODEL),
        scratch_shapes=[
            pltpu.VMEM((2, MIX_TILE, IN_COLS), F32),
            pltpu.VMEM((3, MIX_TILE, D_MODEL), BF16),
            pltpu.VMEM((MIX_TILE + BLK, KV_W), F32),
            pltpu.VMEM((MIX_TILE + BLK, KV_W), F32),
            pltpu.VMEM((GM_W // LANES, BLK, 2 * BLK), BF16),
            pltpu.VMEM((2, 2 * BLK, 4 * BLK), F32),
        ],
        compiler_params=pltpu.CompilerParams(
            dimension_semantics=("arbitrary",), vmem_limit_bytes=VMEM_LIMIT),
        name="mixer",
    )(sinks, x2d, x2d, cos, sin, mix_norm, w_in, q_norm, k_norm,
      gv_norm, ws, gbias, attn_norm, gm_norm, w_out)


def _xattn_kernel(x_ref, xa_norm_ref, wq_ref, q_norm_ref, k_ref, v_ref, wo_ref, out_ref):
    x = x_ref[...]
    h = _rms(x, xa_norm_ref[...]).astype(BF16)
    q = _dot(h, wq_ref[...])
    outs = []
    for hd in range(XA_HEADS):
        cols = slice(hd * XA_DH, (hd + 1) * XA_DH)
        qh = _rms(q[:, cols], q_norm_ref[...]).astype(BF16)
        s = _dot_nt(qh, k_ref[:, cols])
        p = jnp.exp(s - jnp.max(s, axis=-1, keepdims=True))
        inv = 1.0 / jnp.sum(p, axis=-1, keepdims=True)
        outs.append((_dot(p.astype(BF16), v_ref[:, cols]) * inv).astype(BF16))
    out_ref[...] = x + _dot(jnp.concatenate(outs, axis=1), wo_ref[...])


def _xattn(x2d, xa_norm, wq, q_norm, k, v, wo, tiles_per_batch):
    tokens = x2d.shape[0]
    row = pl.BlockSpec((TILE_M, D_MODEL), lambda i: (i, 0))
    kv = pl.BlockSpec((MEM_LEN, D_MODEL), lambda i: (i // tiles_per_batch, 0))
    return pl.pallas_call(
        _xattn_kernel,
        out_shape=jax.ShapeDtypeStruct((tokens, D_MODEL), F32),
        grid=(tokens // TILE_M,),
        in_specs=[row, _const_spec((1, D_MODEL)), _const_spec((D_MODEL, D_MODEL)),
                  _const_spec((1, XA_DH)), kv, kv, _const_spec((D_MODEL, D_MODEL))],
        out_specs=row,
        compiler_params=pltpu.CompilerParams(
            dimension_semantics=("arbitrary",), vmem_limit_bytes=VMEM_LIMIT),
        name="xattn",
    )(x2d, xa_norm, wq, q_norm, k, v, wo)


def _ffn_kernel(tiles_per_batch,
                x_ref, norm_ref, w_up_ref, conv_ref, conv_b_ref, w_down_ref,
                out_ref,
                xp_ref, op_ref, h_ref, carry_ref, g_ref):
    step = pl.program_id(0)
    nb = FFN_TILE // SUBLANES

    @pl.when((step % tiles_per_batch) == 0)
    def _reset_carry():
        carry_ref[...] = jnp.zeros(carry_ref.shape, F32)

    n_lane_groups = D_MODEL // LANES
    pitch = nb + SUBLANES
    for i in range(nb):
        for lg in range(n_lane_groups):
            xp_ref[lg, pl.ds(i, SUBLANES, stride=pitch), :] = (
                x_ref[i * SUBLANES:(i + 1) * SUBLANES, lg * LANES:(lg + 1) * LANES])
    xp = jnp.concatenate(
        [jnp.concatenate([xp_ref[lg, r * pitch:r * pitch + nb, :] for r in range(SUBLANES)], axis=0)
         for lg in range(n_lane_groups)], axis=1)
    h_ref[...] = _rms(xp, norm_ref[...]).astype(BF16)

    first_row = lax.broadcasted_iota(jnp.int32, (SUBLANES, FF_CHUNK), 0) == 0

    def shift_stream(blk, carried):
        rolled = pltpu.roll(blk, 1, 0)
        head = jnp.where(first_row, carried, rolled[0:SUBLANES])
        return jnp.concatenate([head, rolled[SUBLANES:]], axis=0), rolled[0:SUBLANES]

    def conv(a, cols):
        s6, c6 = shift_stream(a[6 * nb:7 * nb], carry_ref[0, :, cols])
        s7, c7 = shift_stream(a[7 * nb:8 * nb], carry_ref[1, :, cols])
        carry_ref[0, :, cols] = c6
        carry_ref[1, :, cols] = c7
        back1 = jnp.concatenate([s7, a[0:7 * nb]], axis=0)
        back2 = jnp.concatenate([s6, s7, a[0:6 * nb]], axis=0)
        return (conv_b_ref[:, cols] + back2 * conv_ref[0:1, cols]
                + back1 * conv_ref[1:2, cols] + a * conv_ref[2:3, cols])

    halves = [None, None]
    for j in range(N_FF_CHUNKS):
        for part in range(2):
            c0 = part * D_FF + j * FF_CHUNK
            cols = slice(c0, c0 + FF_CHUNK)
            halves[part] = conv(_dot(h_ref[...], w_up_ref[:, cols]), cols)
        g_ref[:, j * FF_CHUNK:(j + 1) * FF_CHUNK] = (
            _gelu_tanh(halves[0]) * halves[1]).astype(BF16)
    groups_per_piece = PIECE_COLS // LANES
    for p in range(D_MODEL // PIECE_COLS):
        cols = slice(p * PIECE_COLS, (p + 1) * PIECE_COLS)
        y = _dot(g_ref[...], w_down_ref[:, cols])
        for r in range(SUBLANES):
            rows = slice(r * nb, (r + 1) * nb)
            for k in range(groups_per_piece):
                op_ref[p * groups_per_piece + k, pl.ds(r, nb, stride=SUBLANES), :] = (
                    y[rows, k * LANES:(k + 1) * LANES])
        y_nat = jnp.concatenate(
            [op_ref[p * groups_per_piece + k] for k in range(groups_per_piece)], axis=1)
        out_ref[:, cols] = x_ref[:, cols] + y_nat


def _ffn(x2d, norm, w_up, conv, conv_b, w_down, tiles_per_batch):
    tokens = x2d.shape[0]
    row = pl.BlockSpec((FFN_TILE, D_MODEL), lambda i: (i, 0))
    return pl.pallas_call(
        functools.partial(_ffn_kernel, tiles_per_batch),
        out_shape=jax.ShapeDtypeStruct((tokens, D_MODEL), F32),
        grid=(tokens // FFN_TILE,),
        in_specs=[row, _const_spec((1, D_MODEL)),
                  _const_spec((D_MODEL, 2 * D_FF)),
                  _const_spec((CONV_W, 2 * D_FF)),
                  _const_spec((1, 2 * D_FF)),
                  _const_spec((D_FF, D_MODEL))],
        out_specs=row,
        scratch_shapes=[
            pltpu.VMEM((D_MODEL // LANES, FFN_TILE + SUBLANES * SUBLANES, LANES), F32),
            pltpu.VMEM((D_MODEL // LANES, FFN_TILE, LANES), F32),
            pltpu.VMEM((FFN_TILE, D_MODEL), BF16),
            pltpu.VMEM((2, SUBLANES, 2 * D_FF), F32),
            pltpu.VMEM((FFN_TILE, D_FF), BF16),
        ],
        compiler_params=pltpu.CompilerParams(
            dimension_semantics=("arbitrary",), vmem_limit_bytes=VMEM_LIMIT),
        name="conv_ffn",
    )(x2d, norm, w_up, conv, conv_b, w_down)


def kernel(x, mem, positions, mix_norm, w_in, q_norm, k_norm, attn_sinks, gmlp_v_norm, gmlp_ws, gmlp_bs, attn_out_norm, gmlp_out_norm, w_out, xa_norm, mem_norm, xa_wq, xa_wkv, xa_q_norm, xa_k_norm, xa_wo, ffn_norm, ffn_up, ffn_conv, ffn_conv_b, ffn_down):
    batch, seq, d_model = x.shape
    depth = mix_norm.shape[0]
    assert d_model == D_MODEL and mem.shape[1] == MEM_LEN
    assert seq % MIX_TILE == 0 and seq % FFN_TILE == 0 and seq % TILE_M == 0
    tokens = batch * seq

    inv_freq = 1.0 / (ROPE_THETA ** (jnp.arange(HALF, dtype=F32) * (2.0 / HEAD_DIM)))
    cos, sin = _rope_tables(positions, inv_freq)
    reps = LANES // HEAD_DIM

    x2d = x.reshape(tokens, D_MODEL)
    mem2d = mem.reshape(batch * MEM_LEN, D_MODEL)
    row = lambda a: a.reshape(1, -1)
    for l in range(depth):
        gbias = jnp.repeat(gmlp_bs[l].T, HEAD_DIM, axis=1)
        x2d = _mixer(x2d, cos, sin, attn_sinks[l], row(mix_norm[l]), w_in[l].astype(BF16),
                     row(jnp.tile(q_norm[l], reps)), row(jnp.tile(k_norm[l], reps)),
                     row(gmlp_v_norm[l]), gmlp_ws[l], gbias,
                     row(attn_out_norm[l]), row(gmlp_out_norm[l]), w_out[l].astype(BF16),
                     seq // MIX_TILE)
        k_mem, v_mem = _mem_kv(mem2d, row(mem_norm[l]), xa_wkv[l].astype(BF16),
                               row(xa_k_norm[l]), batch)
        x2d = _xattn(x2d, row(xa_norm[l]), xa_wq[l].astype(BF16), row(xa_q_norm[l]),
                     k_mem, v_mem, xa_wo[l].astype(BF16), seq // TILE_M)
        x2d = _ffn(x2d, row(ffn_norm[l]), ffn_up[l].astype(BF16), ffn_conv[l],
                   row(ffn_conv_b[l]), ffn_down[l].astype(BF16), seq // FFN_TILE)
    return x2d.reshape(batch, seq, D_MODEL)
```

```python
import functools
import math

import jax
import jax.numpy as jnp
from jax import lax
from jax.experimental import pallas as pl
from jax.experimental.pallas import tpu as pltpu

F32 = jnp.float32
BF16 = jnp.bfloat16

LANES = 128
SUBLANES = 8

D_MODEL = 1024
HEAD_DIM = 64
HALF = HEAD_DIM // 2
ATTN_W = 512
KV_W = 128
GM_W = 512
IN_COLS = ATTN_W + 2 * KV_W + 2 * GM_W
BLK = 128
ROPE_THETA = 10000.0
XA_HEADS = 4
XA_DH = 256
MEM_LEN = 256
D_FF = 2816
FF_CHUNK = 256
N_FF_CHUNKS = D_FF // FF_CHUNK
ROPE_ROWS = 8
PIECE_COLS = 256
GZ0 = ATTN_W + 2 * KV_W
CONV_W = 3
EPS = 1e-6
MASK_BIAS = -1e30

SUB_M = 512
MIX_TILE = SUB_M
FFN_TILE = 2 * SUB_M
TILE_M = 2 * SUB_M
VMEM_LIMIT = 56 * 1024 * 1024


def _rms(x, g):
    return x * lax.rsqrt(jnp.mean(x * x, axis=-1, keepdims=True) + EPS) * g


def _gelu_tanh(x):
    c = math.sqrt(2.0 / math.pi)
    half_x = 0.5 * x
    return half_x + half_x * jnp.tanh(x * (c + (c * 0.044715) * (x * x)))


def _dot(a, b):
    return jnp.dot(a, b, preferred_element_type=F32)


def _dot_nt(a, b):
    return lax.dot_general(a, b, (((1,), (1,)), ((), ())), preferred_element_type=F32)


def _const_spec(shape):
    nd = len(shape)
    return pl.BlockSpec(shape, lambda i: (0,) * nd, pipeline_mode=pl.Buffered(1))


def _rope_kernel(inv_freq_ref, pos_ref, cos_ref, sin_ref):
    inv_freq = inv_freq_ref[...]
    for g in range(ROPE_ROWS):
        pos = pos_ref[g:g + 1, :].astype(F32)
        ang = pos * inv_freq
        c = jnp.cos(ang)
        s = jnp.sin(ang)
        reps = LANES // HEAD_DIM
        c_all = jnp.concatenate([c, c] * reps, axis=0)
        s_all = jnp.concatenate([-s, s] * reps, axis=0)
        rows = slice(g * LANES, (g + 1) * LANES)
        cos_ref[rows, :] = c_all.T
        sin_ref[rows, :] = s_all.T


def _rope_tables(positions, inv_freq):
    tokens = positions.size
    pos2d = positions.reshape(tokens // LANES, LANES)
    out = jax.ShapeDtypeStruct((tokens, LANES), F32)
    out_spec = pl.BlockSpec((ROPE_ROWS * LANES, LANES), lambda i: (i, 0))
    return pl.pallas_call(
        _rope_kernel,
        out_shape=(out, out),
        grid=(tokens // (ROPE_ROWS * LANES),),
        in_specs=[_const_spec((HALF, 1)),
                  pl.BlockSpec((ROPE_ROWS, LANES), lambda i: (i, 0))],
        out_specs=(out_spec, out_spec),
        compiler_params=pltpu.CompilerParams(dimension_semantics=("arbitrary",)),
        name="rope_tables",
    )(inv_freq.reshape(HALF, 1), pos2d)


def _mem_kv_kernel(mem_ref, mem_norm_ref, wkv_ref, k_norm_ref, k_ref, v_ref):
    m = _rms(mem_ref[...], mem_norm_ref[...]).astype(BF16)
    kv = _dot(m, wkv_ref[...])
    scale = 1.0 / math.sqrt(XA_DH)
    for h in range(XA_HEADS):
        cols = slice(h * XA_DH, (h + 1) * XA_DH)
        k_ref[:, cols] = (_rms(kv[:, cols], k_norm_ref[...]) * scale).astype(BF16)
    v_ref[...] = kv[:, D_MODEL:].astype(BF16)


def _mem_kv(mem2d, mem_norm, wkv, k_norm, batch):
    out = jax.ShapeDtypeStruct((batch * MEM_LEN, D_MODEL), BF16)
    blk = pl.BlockSpec((MEM_LEN, D_MODEL), lambda b: (b, 0))
    return pl.pallas_call(
        _mem_kv_kernel,
        out_shape=(out, out),
        grid=(batch,),
        in_specs=[blk, _const_spec((1, D_MODEL)), _const_spec((D_MODEL, 2 * D_MODEL)),
                  _const_spec((1, XA_DH))],
        out_specs=(blk, blk),
        compiler_params=pltpu.CompilerParams(
            dimension_semantics=("arbitrary",), vmem_limit_bytes=VMEM_LIMIT),
        name="mem_kv",
    )(mem2d, mem_norm, wkv, k_norm)


def _mixer_kernel(tiles_per_batch,
                  sinks_ref, xa_ref, xc_ref, cos_ref, sin_ref, mix_norm_ref, w_in_ref,
                  q_norm_ref, k_norm_ref, gv_norm_ref, ws_ref, gbias_ref,
                  attn_norm_ref, gm_norm_ref, w_out_ref,
                  out_ref,
                  proj2_ref, hy_ref,
                  kbuf_ref, vbuf_ref, wcat_ref, bias_ref):
    step = pl.program_id(0)
    first_tile = ((step + tiles_per_batch - 1) % tiles_per_batch) == 0
    n_blocks = MIX_TILE // BLK
    proj_refs = (proj2_ref.at[0], proj2_ref.at[1])
    h_plane, y_plane = 0, (1, 2)
    y_refs = (hy_ref.at[y_plane[0]], hy_ref.at[y_plane[1]])

    @pl.when(step == 0)
    def _build_constants():
        t = lax.broadcasted_iota(jnp.int32, (BLK, BLK), 0)
        s = lax.broadcasted_iota(jnp.int32, (BLK, BLK), 1)
        for j in range(GM_W // LANES):
            w_a = jnp.where(s <= t, ws_ref[2 * j], 0.0)
            w_b = jnp.where(s <= t, ws_ref[2 * j + 1], 0.0)
            wcat_ref[j] = jnp.concatenate([w_a, w_b], axis=1).astype(BF16)
        qi = lax.broadcasted_iota(jnp.int32, (2 * BLK, 4 * BLK), 0) & (BLK - 1)
        kj = lax.broadcasted_iota(jnp.int32, (2 * BLK, 4 * BLK), 1) & (2 * BLK - 1)
        cur = (kj >= BLK) & ((kj - BLK) <= qi)
        prev = (kj < BLK) & (kj > qi)
        bias_ref[0] = jnp.where(cur | prev, 0.0, MASK_BIAS)
        bias_ref[1] = jnp.where(cur, 0.0, MASK_BIAS)
        proj2_ref[1] = jnp.zeros((MIX_TILE, IN_COLS), F32)
        hy_ref[y_plane[1]] = jnp.zeros((MIX_TILE, D_MODEL), BF16)

    @pl.when(first_tile)
    def _reset_carry():
        kbuf_ref[0:BLK, :] = jnp.zeros((BLK, KV_W), F32)
        vbuf_ref[0:BLK, :] = jnp.zeros((BLK, KV_W), F32)

    lane = lax.broadcasted_iota(jnp.int32, (BLK, LANES), 1)
    low_head = lane < HEAD_DIM
    first_half = (lane & (HEAD_DIM - 1)) < HALF
    r = lax.broadcasted_iota(jnp.int32, (LANES, LANES), 0)
    c = lax.broadcasted_iota(jnp.int32, (LANES, LANES), 1)
    head_ones = jnp.where((r < HEAD_DIM) == (c < HEAD_DIM), 1.0, 0.0).astype(BF16)
    lane2 = lax.broadcasted_iota(jnp.int32, (2 * BLK, LANES), 1)
    low_head2 = lane2 < HEAD_DIM

    def head_norm_rope(t, gain, cos, sin):
        ss = _dot((t * t).astype(BF16), head_ones)
        tn = t * lax.rsqrt(ss * (1.0 / HEAD_DIM) + EPS) * gain
        rot = jnp.where(first_half, pltpu.roll(tn, LANES - HALF, 1), pltpu.roll(tn, HALF, 1))
        return tn * cos + rot * sin

    def block(b, proj_ref, y_ref):
        r0 = b * BLK
        rows = pl.ds(r0, BLK)
        cos = cos_ref[rows, :]
        sin = sin_ref[rows, :]

        k = head_norm_rope(proj_ref[rows, ATTN_W:ATTN_W + KV_W], k_norm_ref[...], cos, sin)
        kbuf_ref[pl.ds(r0 + BLK, BLK), :] = k
        vbuf_ref[pl.ds(r0 + BLK, BLK), :] = proj_ref[rows, ATTN_W + KV_W:ATTN_W + 2 * KV_W]
        kcat = kbuf_ref[pl.ds(r0, 2 * BLK), :]
        vcat = vbuf_ref[pl.ds(r0, 2 * BLK), :]
        kswp = pltpu.roll(kcat, HEAD_DIM, 1)
        vswp = pltpu.roll(vcat, HEAD_DIM, 1)

        bias = bias_ref[first_tile.astype(jnp.int32)] if b == 0 else bias_ref[0]

        attn_parts = []
        for g in range(2):
            k_lo, k_hi = (kcat, kswp) if g == 0 else (kswp, kcat)
            v_lo, v_hi = (vcat, vswp) if g == 0 else (vswp, vcat)
            k_big = jnp.concatenate([jnp.where(low_head2, k_lo, 0.0),
                                     jnp.where(low_head2, 0.0, k_hi)], axis=0).astype(BF16)
            v_big = jnp.concatenate([jnp.where(low_head2, v_lo, 0.0),
                                     jnp.where(low_head2, 0.0, v_hi)], axis=0).astype(BF16)
            q_pairs = []
            for pair in (2 * g, 2 * g + 1):
                qcols = slice(pair * LANES, (pair + 1) * LANES)
                q = head_norm_rope(proj_ref[rows, qcols], q_norm_ref[...], cos, sin)
                q_pairs.append((q * (1.0 / math.sqrt(HEAD_DIM))).astype(BF16))
            qs = jnp.concatenate(q_pairs, axis=0)
            s = _dot_nt(qs, k_big) + bias
            yield
            p_rows, inv_rows = [], []
            for rg in range(2):
                p_cols, inv_cols = [], []
                for half in range(2):
                    sink = sinks_ref[4 * g + 2 * rg + half]
                    sh = s[rg * BLK:(rg + 1) * BLK, half * 2 * BLK:(half + 1) * 2 * BLK]
                    m = jnp.maximum(jnp.max(sh, axis=-1, keepdims=True), sink)
                    p = jnp.exp(sh - m)
                    denom = jnp.sum(p, axis=-1, keepdims=True) + jnp.exp(sink - m)
                    p_cols.append(p.astype(BF16))
                    inv_cols.append(1.0 / denom)
                p_rows.append(jnp.concatenate(p_cols, axis=1))
                inv_rows.append(jnp.where(low_head, inv_cols[0], inv_cols[1]))
            o = _dot(jnp.concatenate(p_rows, axis=0), v_big)
            attn_parts.append(o[0:BLK] * inv_rows[0])
            attn_parts.append(o[BLK:2 * BLK] * inv_rows[1])
        attn = jnp.concatenate(attn_parts, axis=1)
        y_ref[rows, 0:ATTN_W] = _rms(attn, attn_norm_ref[...]).astype(BF16)
        yield

        gu = _gelu_tanh(proj_ref[rows, GZ0:GZ0 + GM_W])
        gv = _rms(_gelu_tanh(proj_ref[rows, GZ0 + GM_W:GZ0 + 2 * GM_W]), gv_norm_ref[...])
        mixed = []
        for j in range(GM_W // LANES):
            vb = gv[:, j * LANES:(j + 1) * LANES]
            v_bd = jnp.concatenate([jnp.where(low_head, vb, 0.0),
                                    jnp.where(low_head, 0.0, vb)], axis=0).astype(BF16)
            mixed.append(_dot(wcat_ref[j], v_bd))
        gm = gu * (jnp.concatenate(mixed, axis=1) + gbias_ref[...])
        y_ref[rows, ATTN_W:ATTN_W + GM_W] = _rms(gm, gm_norm_ref[...]).astype(BF16)

    def body(cur, prev):
        def in_piece(p):
            cols = slice(p * PIECE_COLS, (p + 1) * PIECE_COLS)
            proj_refs[cur][:, cols] = _dot(hy_ref[h_plane], w_in_ref[:, cols])

        def out_piece(p):
            cols = slice(p * PIECE_COLS, (p + 1) * PIECE_COLS)
            out_ref[:, cols] = xc_ref[:, cols] + _dot(hy_ref[y_plane[prev]], w_out_ref[:, cols])

        hy_ref[h_plane] = _rms(xa_ref[...], mix_norm_ref[...]).astype(BF16)
        pieces = ([functools.partial(out_piece, p) for p in range(D_MODEL // PIECE_COLS)]
                  + [functools.partial(in_piece, p) for p in range(IN_COLS // PIECE_COLS)])
        pieces.reverse()
        for b in range(n_blocks):
            if pieces:
                pieces.pop()()
            for _ in block(b, proj_refs[prev], y_refs[cur]):
                if pieces:
                    pieces.pop()()
        while pieces:
            pieces.pop()()

    for parity in range(2):
        pl.when((step % 2) == parity)(functools.partial(body, parity, 1 - parity))

    kbuf_ref[0:BLK, :] = kbuf_ref[MIX_TILE:MIX_TILE + BLK, :]
    vbuf_ref[0:BLK, :] = vbuf_ref[MIX_TILE:MIX_TILE + BLK, :]


def _mixer(x2d, cos, sin, sinks, mix_norm, w_in, q_norm, k_norm, gv_norm, ws, gbias,
           attn_norm, gm_norm, w_out, tiles_per_batch):
    tokens = x2d.shape[0]
    n_tiles = tokens // MIX_TILE

    def stage(lag, w):
        return pl.BlockSpec((MIX_TILE, w), lambda i: (jnp.clip(i - lag, 0, n_tiles - 1), 0))

    return pl.pallas_call(
        functools.partial(_mixer_kernel, tiles_per_batch),
        out_shape=jax.ShapeDtypeStruct((tokens, D_MODEL), F32),
        grid=(n_tiles + 2,),
        in_specs=[pl.BlockSpec(memory_space=pltpu.SMEM),
                  stage(0, D_MODEL), stage(2, D_MODEL), stage(1, LANES), stage(1, LANES),
                  _const_spec((1, D_MODEL)), _const_spec((D_MODEL, IN_COLS)),
                  _const_spec((1, LANES)), _const_spec((1, LANES)),
                  _const_spec((1, GM_W)), _const_spec((GM_W // HEAD_DIM, BLK, BLK)),
                  _const_spec((BLK, GM_W)),
                  _const_spec((1, ATTN_W)), _const_spec((1, GM_W)),
                  _const_spec((D_MODEL, D_MODEL))],
        out_specs=stage(2, D_MODEL),
        scratch_shapes=[
            pltpu.VMEM((2, MIX_TILE, IN_COLS), F32),
            pltpu.VMEM((3, MIX_TILE, D_MODEL), BF16),
            pltpu.VMEM((MIX_TILE + BLK, KV_W), F32),
            pltpu.VMEM((MIX_TILE + BLK, KV_W), F32),
            pltpu.VMEM((GM_W // LANES, BLK, 2 * BLK), BF16),
            pltpu.VMEM((2, 2 * BLK, 4 * BLK), F32),
        ],
        compiler_params=pltpu.CompilerParams(
            dimension_semantics=("arbitrary",), vmem_limit_bytes=VMEM_LIMIT),
        name="mixer",
    )(sinks, x2d, x2d, cos, sin, mix_norm, w_in, q_norm, k_norm,
      gv_norm, ws, gbias, attn_norm, gm_norm, w_out)


def _xattn_kernel(x_ref, xa_norm_ref, wq_ref, q_norm_ref, k_ref, v_ref, wo_ref,
                  w_up_f32_ref, w_down_f32_ref,
                  out_ref, w_up_ref, w_down_ref):
    w_up_ref[...] = w_up_f32_ref[...].astype(BF16)
    w_down_ref[...] = w_down_f32_ref[...].astype(BF16)

    x = x_ref[...]
    h = _rms(x, xa_norm_ref[...]).astype(BF16)
    q = _dot(h, wq_ref[...])
    outs = []
    for hd in range(XA_HEADS):
        cols = slice(hd * XA_DH, (hd + 1) * XA_DH)
        qh = _rms(q[:, cols], q_norm_ref[...]).astype(BF16)
        s = _dot_nt(qh, k_ref[:, cols])
        p = jnp.exp(s - jnp.max(s, axis=-1, keepdims=True))
        inv = 1.0 / jnp.sum(p, axis=-1, keepdims=True)
        outs.append((_dot(p.astype(BF16), v_ref[:, cols]) * inv).astype(BF16))
    out_ref[...] = x + _dot(jnp.concatenate(outs, axis=1), wo_ref[...])


def _passenger_spec(shape, n_steps):
    rows, cols = shape
    n_blocks = next(n for n in range(n_steps, 0, -1)
                    if n_steps % n == 0 and rows % n == 0 and (rows // n) % (2 * SUBLANES) == 0)
    visits = n_steps // n_blocks
    return pl.BlockSpec((rows // n_blocks, cols), lambda i: (i // visits, 0))


def _xattn(x2d, xa_norm, wq, q_norm, k, v, wo, tiles_per_batch, ffn_up, ffn_down):
    tokens = x2d.shape[0]
    n_tiles = tokens // TILE_M
    row = pl.BlockSpec((TILE_M, D_MODEL), lambda i: (i, 0))
    kv = pl.BlockSpec((MEM_LEN, D_MODEL), lambda i: (i // tiles_per_batch, 0))
    up_spec = _passenger_spec(ffn_up.shape, n_tiles)
    down_spec = _passenger_spec(ffn_down.shape, n_tiles)
    return pl.pallas_call(
        _xattn_kernel,
        out_shape=(jax.ShapeDtypeStruct((tokens, D_MODEL), F32),
                   jax.ShapeDtypeStruct(ffn_up.shape, BF16),
                   jax.ShapeDtypeStruct(ffn_down.shape, BF16)),
        grid=(n_tiles,),
        in_specs=[row, _const_spec((1, D_MODEL)), _const_spec((D_MODEL, D_MODEL)),
                  _const_spec((1, XA_DH)), kv, kv, _const_spec((D_MODEL, D_MODEL)),
                  up_spec, down_spec],
        out_specs=(row, up_spec, down_spec),
        compiler_params=pltpu.CompilerParams(
            dimension_semantics=("arbitrary",), vmem_limit_bytes=VMEM_LIMIT),
        name="xattn",
    )(x2d, xa_norm, wq, q_norm, k, v, wo, ffn_up, ffn_down)


def _ffn_kernel(tiles_per_batch,
                x_ref, norm_ref, w_up_ref, conv_ref, conv_b_ref, w_down_ref,
                out_ref,
                xp_ref, op_ref, h_ref, carry_ref, g_ref):
    step = pl.program_id(0)
    nb = FFN_TILE // SUBLANES

    @pl.when((step % tiles_per_batch) == 0)
    def _reset_carry():
        carry_ref[...] = jnp.zeros(carry_ref.shape, F32)

    n_lane_groups = D_MODEL // LANES
    pitch = nb + SUBLANES
    for i in range(nb):
        for lg in range(n_lane_groups):
            xp_ref[lg, pl.ds(i, SUBLANES, stride=pitch), :] = (
                x_ref[i * SUBLANES:(i + 1) * SUBLANES, lg * LANES:(lg + 1) * LANES])
    xp = jnp.concatenate(
        [jnp.concatenate([xp_ref[lg, r * pitch:r * pitch + nb, :] for r in range(SUBLANES)], axis=0)
         for lg in range(n_lane_groups)], axis=1)
    h_ref[...] = _rms(xp, norm_ref[...]).astype(BF16)

    first_row = lax.broadcasted_iota(jnp.int32, (SUBLANES, FF_CHUNK), 0) == 0

    def shift_stream(blk, carried):
        rolled = pltpu.roll(blk, 1, 0)
        head = jnp.where(first_row, carried, rolled[0:SUBLANES])
        return jnp.concatenate([head, rolled[SUBLANES:]], axis=0), rolled[0:SUBLANES]

    def conv(a, cols):
        s6, c6 = shift_stream(a[6 * nb:7 * nb], carry_ref[0, :, cols])
        s7, c7 = shift_stream(a[7 * nb:8 * nb], carry_ref[1, :, cols])
        carry_ref[0, :, cols] = c6
        carry_ref[1, :, cols] = c7
        back1 = jnp.concatenate([s7, a[0:7 * nb]], axis=0)
        back2 = jnp.concatenate([s6, s7, a[0:6 * nb]], axis=0)
        return (conv_b_ref[:, cols] + back2 * conv_ref[0:1, cols]
                + back1 * conv_ref[1:2, cols] + a * conv_ref[2:3, cols])

    halves = [None, None]
    for j in range(N_FF_CHUNKS):
        for part in range(2):
            c0 = part * D_FF + j * FF_CHUNK
            cols = slice(c0, c0 + FF_CHUNK)
            halves[part] = conv(_dot(h_ref[...], w_up_ref[:, cols]), cols)
        g_ref[:, j * FF_CHUNK:(j + 1) * FF_CHUNK] = (
            _gelu_tanh(halves[0]) * halves[1]).astype(BF16)
    groups_per_piece = PIECE_COLS // LANES
    for p in range(D_MODEL // PIECE_COLS):
        cols = slice(p * PIECE_COLS, (p + 1) * PIECE_COLS)
        y = _dot(g_ref[...], w_down_ref[:, cols])
        for r in range(SUBLANES):
            rows = slice(r * nb, (r + 1) * nb)
            for k in range(groups_per_piece):
                op_ref[p * groups_per_piece + k, pl.ds(r, nb, stride=SUBLANES), :] = (
                    y[rows, k * LANES:(k + 1) * LANES])
        y_nat = jnp.concatenate(
            [op_ref[p * groups_per_piece + k] for k in range(groups_per_piece)], axis=1)
        out_ref[:, cols] = x_ref[:, cols] + y_nat


def _ffn(x2d, norm, w_up, conv, conv_b, w_down, tiles_per_batch):
    tokens = x2d.shape[0]
    row = pl.BlockSpec((FFN_TILE, D_MODEL), lambda i: (i, 0))
    return pl.pallas_call(
        functools.partial(_ffn_kernel, tiles_per_batch),
        out_shape=jax.ShapeDtypeStruct((tokens, D_MODEL), F32),
        grid=(tokens // FFN_TILE,),
        in_specs=[row, _const_spec((1, D_MODEL)),
                  _const_spec((D_MODEL, 2 * D_FF)),
                  _const_spec((CONV_W, 2 * D_FF)),
                  _const_spec((1, 2 * D_FF)),
                  _const_spec((D_FF, D_MODEL))],
        out_specs=row,
        scratch_shapes=[
            pltpu.VMEM((D_MODEL // LANES, FFN_TILE + SUBLANES * SUBLANES, LANES), F32),
            pltpu.VMEM((D_MODEL // LANES, FFN_TILE, LANES), F32),
            pltpu.VMEM((FFN_TILE, D_MODEL), BF16),
            pltpu.VMEM((2, SUBLANES, 2 * D_FF), F32),
            pltpu.VMEM((FFN_TILE, D_FF), BF16),
        ],
        compiler_params=pltpu.CompilerParams(
            dimension_semantics=("arbitrary",), vmem_limit_bytes=VMEM_LIMIT),
        name="conv_ffn",
    )(x2d, norm, w_up, conv, conv_b, w_down)


def kernel(x, mem, positions, mix_norm, w_in, q_norm, k_norm, attn_sinks, gmlp_v_norm, gmlp_ws, gmlp_bs, attn_out_norm, gmlp_out_norm, w_out, xa_norm, mem_norm, xa_wq, xa_wkv, xa_q_norm, xa_k_norm, xa_wo, ffn_norm, ffn_up, ffn_conv, ffn_conv_b, ffn_down):
    batch, seq, d_model = x.shape
    depth = mix_norm.shape[0]
    assert d_model == D_MODEL and mem.shape[1] == MEM_LEN
    assert seq % MIX_TILE == 0 and seq % FFN_TILE == 0 and seq % TILE_M == 0
    tokens = batch * seq

    inv_freq = 1.0 / (ROPE_THETA ** (jnp.arange(HALF, dtype=F32) * (2.0 / HEAD_DIM)))
    cos, sin = _rope_tables(positions, inv_freq)
    reps = LANES // HEAD_DIM

    x2d = x.reshape(tokens, D_MODEL)
    mem2d = mem.reshape(batch * MEM_LEN, D_MODEL)
    row = lambda a: a.reshape(1, -1)
    for l in range(depth):
        gbias = jnp.repeat(gmlp_bs[l].T, HEAD_DIM, axis=1)
        x2d = _mixer(x2d, cos, sin, attn_sinks[l], row(mix_norm[l]), w_in[l].astype(BF16),
                     row(jnp.tile(q_norm[l], reps)), row(jnp.tile(k_norm[l], reps)),
                     row(gmlp_v_norm[l]), gmlp_ws[l], gbias,
                     row(attn_out_norm[l]), row(gmlp_out_norm[l]), w_out[l].astype(BF16),
                     seq // MIX_TILE)
        k_mem, v_mem = _mem_kv(mem2d, row(mem_norm[l]), xa_wkv[l].astype(BF16),
                               row(xa_k_norm[l]), batch)
        x2d, w_up, w_down = _xattn(x2d, row(xa_norm[l]), xa_wq[l].astype(BF16), row(xa_q_norm[l]),
                                   k_mem, v_mem, xa_wo[l].astype(BF16), seq // TILE_M,
                                   ffn_up[l], ffn_down[l])
        x2d = _ffn(x2d, row(ffn_norm[l]), w_up, ffn_conv[l],
                   row(ffn_conv_b[l]), w_down, seq // FFN_TILE)
    return x2d.reshape(batch, seq, D_MODEL)
```

```python
import functools
import math

import jax
import jax.numpy as jnp
from jax import lax
from jax.experimental import pallas as pl
from jax.experimental.pallas import tpu as pltpu

F32 = jnp.float32
BF16 = jnp.bfloat16

LANES = 128
SUBLANES = 8

D_MODEL = 1024
HEAD_DIM = 64
HALF = HEAD_DIM // 2
ATTN_W = 512
KV_W = 128
GM_W = 512
IN_COLS = ATTN_W + 2 * KV_W + 2 * GM_W
BLK = 128
ROPE_THETA = 10000.0
XA_HEADS = 4
XA_DH = 256
MEM_LEN = 256
D_FF = 2816
FF_CHUNK = 256
N_FF_CHUNKS = D_FF // FF_CHUNK
ROPE_ROWS = 8
PIECE_COLS = 256
GZ0 = ATTN_W + 2 * KV_W
CONV_W = 3
EPS = 1e-6
MASK_BIAS = -1e30

SUB_M = 512
MIX_TILE = SUB_M
FFN_TILE = 2 * SUB_M
TILE_M = 2 * SUB_M
VMEM_LIMIT = 56 * 1024 * 1024


def _rms(x, g):
    return x * lax.rsqrt(jnp.mean(x * x, axis=-1, keepdims=True) + EPS) * g


def _gelu_tanh(x):
    c = math.sqrt(2.0 / math.pi)
    half_x = 0.5 * x
    return half_x + half_x * jnp.tanh(x * (c + (c * 0.044715) * (x * x)))


def _dot(a, b):
    return jnp.dot(a, b, preferred_element_type=F32)


def _dot_nt(a, b):
    return lax.dot_general(a, b, (((1,), (1,)), ((), ())), preferred_element_type=F32)


def _const_spec(shape):
    nd = len(shape)
    return pl.BlockSpec(shape, lambda i: (0,) * nd, pipeline_mode=pl.Buffered(1))


def _rope_kernel(n_passengers, inv_freq_ref, pos_ref, *refs):
    f32_refs = refs[:n_passengers]
    cos_ref, sin_ref = refs[n_passengers:n_passengers + 2]
    bf16_refs = refs[n_passengers + 2:]
    for src, dst in zip(f32_refs, bf16_refs):
        dst[...] = src[...].astype(BF16)

    inv_freq = inv_freq_ref[...]
    for g in range(ROPE_ROWS):
        pos = pos_ref[g:g + 1, :].astype(F32)
        ang = pos * inv_freq
        c = jnp.cos(ang)
        s = jnp.sin(ang)
        reps = LANES // HEAD_DIM
        c_all = jnp.concatenate([c, c] * reps, axis=0)
        s_all = jnp.concatenate([-s, s] * reps, axis=0)
        rows = slice(g * LANES, (g + 1) * LANES)
        cos_ref[rows, :] = c_all.T
        sin_ref[rows, :] = s_all.T


def _rope_tables(positions, inv_freq, weights):
    tokens = positions.size
    n_steps = tokens // (ROPE_ROWS * LANES)
    pos2d = positions.reshape(tokens // LANES, LANES)
    out = jax.ShapeDtypeStruct((tokens, LANES), F32)
    out_spec = pl.BlockSpec((ROPE_ROWS * LANES, LANES), lambda i: (i, 0))
    w_specs = [_passenger_spec(w.shape, n_steps) for w in weights]
    cos, sin, *w_bf16 = pl.pallas_call(
        functools.partial(_rope_kernel, len(weights)),
        out_shape=(out, out) + tuple(jax.ShapeDtypeStruct(w.shape, BF16) for w in weights),
        grid=(n_steps,),
        in_specs=[_const_spec((HALF, 1)),
                  pl.BlockSpec((ROPE_ROWS, LANES), lambda i: (i, 0))] + w_specs,
        out_specs=(out_spec, out_spec) + tuple(w_specs),
        compiler_params=pltpu.CompilerParams(dimension_semantics=("arbitrary",)),
        name="rope_tables",
    )(inv_freq.reshape(HALF, 1), pos2d, *weights)
    return cos, sin, w_bf16


def _mem_kv_kernel(mem_ref, mem_norm_ref, wkv_ref, k_norm_ref, k_ref, v_ref):
    m = _rms(mem_ref[...], mem_norm_ref[...]).astype(BF16)
    kv = _dot(m, wkv_ref[...])
    scale = 1.0 / math.sqrt(XA_DH)
    for h in range(XA_HEADS):
        cols = slice(h * XA_DH, (h + 1) * XA_DH)
        k_ref[:, cols] = (_rms(kv[:, cols], k_norm_ref[...]) * scale).astype(BF16)
    v_ref[...] = kv[:, D_MODEL:].astype(BF16)


def _mem_kv(mem2d, mem_norm, wkv, k_norm, batch):
    out = jax.ShapeDtypeStruct((batch * MEM_LEN, D_MODEL), BF16)
    blk = pl.BlockSpec((MEM_LEN, D_MODEL), lambda b: (b, 0))
    return pl.pallas_call(
        _mem_kv_kernel,
        out_shape=(out, out),
        grid=(batch,),
        in_specs=[blk, _const_spec((1, D_MODEL)), _const_spec((D_MODEL, 2 * D_MODEL)),
                  _const_spec((1, XA_DH))],
        out_specs=(blk, blk),
        compiler_params=pltpu.CompilerParams(
            dimension_semantics=("arbitrary",), vmem_limit_bytes=VMEM_LIMIT),
        name="mem_kv",
    )(mem2d, mem_norm, wkv, k_norm)


def _mixer_kernel(tiles_per_batch,
                  sinks_ref, xa_ref, xc_ref, cos_ref, sin_ref, mix_norm_ref, w_in_ref,
                  q_norm_ref, k_norm_ref, gv_norm_ref, ws_ref, gbias_ref,
                  attn_norm_ref, gm_norm_ref, w_out_ref,
                  out_ref,
                  proj2_ref, hy_ref,
                  kbuf_ref, vbuf_ref, wcat_ref, bias_ref):
    step = pl.program_id(0)
    first_tile = ((step + tiles_per_batch - 1) % tiles_per_batch) == 0
    n_blocks = MIX_TILE // BLK
    proj_refs = (proj2_ref.at[0], proj2_ref.at[1])
    h_plane, y_plane = 0, (1, 2)
    y_refs = (hy_ref.at[y_plane[0]], hy_ref.at[y_plane[1]])

    @pl.when(step == 0)
    def _build_constants():
        t = lax.broadcasted_iota(jnp.int32, (BLK, BLK), 0)
        s = lax.broadcasted_iota(jnp.int32, (BLK, BLK), 1)
        for j in range(GM_W // LANES):
            w_a = jnp.where(s <= t, ws_ref[2 * j], 0.0)
            w_b = jnp.where(s <= t, ws_ref[2 * j + 1], 0.0)
            wcat_ref[j] = jnp.concatenate([w_a, w_b], axis=1).astype(BF16)
        qi = lax.broadcasted_iota(jnp.int32, (2 * BLK, 4 * BLK), 0) & (BLK - 1)
        kj = lax.broadcasted_iota(jnp.int32, (2 * BLK, 4 * BLK), 1) & (2 * BLK - 1)
        cur = (kj >= BLK) & ((kj - BLK) <= qi)
        prev = (kj < BLK) & (kj > qi)
        bias_ref[0] = jnp.where(cur | prev, 0.0, MASK_BIAS)
        bias_ref[1] = jnp.where(cur, 0.0, MASK_BIAS)
        proj2_ref[1] = jnp.zeros((MIX_TILE, IN_COLS), F32)
        hy_ref[y_plane[1]] = jnp.zeros((MIX_TILE, D_MODEL), BF16)

    @pl.when(first_tile)
    def _reset_carry():
        kbuf_ref[0:BLK, :] = jnp.zeros((BLK, KV_W), F32)
        vbuf_ref[0:BLK, :] = jnp.zeros((BLK, KV_W), F32)

    lane = lax.broadcasted_iota(jnp.int32, (BLK, LANES), 1)
    low_head = lane < HEAD_DIM
    first_half = (lane & (HEAD_DIM - 1)) < HALF
    r = lax.broadcasted_iota(jnp.int32, (LANES, LANES), 0)
    c = lax.broadcasted_iota(jnp.int32, (LANES, LANES), 1)
    head_ones = jnp.where((r < HEAD_DIM) == (c < HEAD_DIM), 1.0, 0.0).astype(BF16)
    lane2 = lax.broadcasted_iota(jnp.int32, (2 * BLK, LANES), 1)
    low_head2 = lane2 < HEAD_DIM

    def head_norm_rope(t, gain, cos, sin):
        ss = _dot((t * t).astype(BF16), head_ones)
        tn = t * lax.rsqrt(ss * (1.0 / HEAD_DIM) + EPS) * gain
        rot = jnp.where(first_half, pltpu.roll(tn, LANES - HALF, 1), pltpu.roll(tn, HALF, 1))
        return tn * cos + rot * sin

    def block(b, proj_ref, y_ref):
        r0 = b * BLK
        rows = pl.ds(r0, BLK)
        cos = cos_ref[rows, :]
        sin = sin_ref[rows, :]

        k = head_norm_rope(proj_ref[rows, ATTN_W:ATTN_W + KV_W], k_norm_ref[...], cos, sin)
        kbuf_ref[pl.ds(r0 + BLK, BLK), :] = k
        vbuf_ref[pl.ds(r0 + BLK, BLK), :] = proj_ref[rows, ATTN_W + KV_W:ATTN_W + 2 * KV_W]
        kcat = kbuf_ref[pl.ds(r0, 2 * BLK), :]
        vcat = vbuf_ref[pl.ds(r0, 2 * BLK), :]
        kswp = pltpu.roll(kcat, HEAD_DIM, 1)
        vswp = pltpu.roll(vcat, HEAD_DIM, 1)

        bias = bias_ref[first_tile.astype(jnp.int32)] if b == 0 else bias_ref[0]

        attn_parts = []
        for g in range(2):
            k_lo, k_hi = (kcat, kswp) if g == 0 else (kswp, kcat)
            v_lo, v_hi = (vcat, vswp) if g == 0 else (vswp, vcat)
            k_big = jnp.concatenate([jnp.where(low_head2, k_lo, 0.0),
                                     jnp.where(low_head2, 0.0, k_hi)], axis=0).astype(BF16)
            v_big = jnp.concatenate([jnp.where(low_head2, v_lo, 0.0),
                                     jnp.where(low_head2, 0.0, v_hi)], axis=0).astype(BF16)
            q_pairs = []
            for pair in (2 * g, 2 * g + 1):
                qcols = slice(pair * LANES, (pair + 1) * LANES)
                q = head_norm_rope(proj_ref[rows, qcols], q_norm_ref[...], cos, sin)
                q_pairs.append((q * (1.0 / math.sqrt(HEAD_DIM))).astype(BF16))
            qs = jnp.concatenate(q_pairs, axis=0)
            s = _dot_nt(qs, k_big) + bias
            yield
            p_rows, inv_rows = [], []
            for rg in range(2):
                p_cols, inv_cols = [], []
                for half in range(2):
                    sink = sinks_ref[4 * g + 2 * rg + half]
                    sh = s[rg * BLK:(rg + 1) * BLK, half * 2 * BLK:(half + 1) * 2 * BLK]
                    m = jnp.maximum(jnp.max(sh, axis=-1, keepdims=True), sink)
                    p = jnp.exp(sh - m)
                    denom = jnp.sum(p, axis=-1, keepdims=True) + jnp.exp(sink - m)
                    p_cols.append(p.astype(BF16))
                    inv_cols.append(1.0 / denom)
                p_rows.append(jnp.concatenate(p_cols, axis=1))
                inv_rows.append(jnp.where(low_head, inv_cols[0], inv_cols[1]))
            o = _dot(jnp.concatenate(p_rows, axis=0), v_big)
            attn_parts.append(o[0:BLK] * inv_rows[0])
            attn_parts.append(o[BLK:2 * BLK] * inv_rows[1])
        attn = jnp.concatenate(attn_parts, axis=1)
        y_ref[rows, 0:ATTN_W] = _rms(attn, attn_norm_ref[...]).astype(BF16)
        yield

        gu = _gelu_tanh(proj_ref[rows, GZ0:GZ0 + GM_W])
        gv = _rms(_gelu_tanh(proj_ref[rows, GZ0 + GM_W:GZ0 + 2 * GM_W]), gv_norm_ref[...])
        mixed = []
        for j in range(GM_W // LANES):
            vb = gv[:, j * LANES:(j + 1) * LANES]
            v_bd = jnp.concatenate([jnp.where(low_head, vb, 0.0),
                                    jnp.where(low_head, 0.0, vb)], axis=0).astype(BF16)
            mixed.append(_dot(wcat_ref[j], v_bd))
        gm = gu * (jnp.concatenate(mixed, axis=1) + gbias_ref[...])
        y_ref[rows, ATTN_W:ATTN_W + GM_W] = _rms(gm, gm_norm_ref[...]).astype(BF16)

    def body(cur, prev):
        def in_piece(p):
            cols = slice(p * PIECE_COLS, (p + 1) * PIECE_COLS)
            proj_refs[cur][:, cols] = _dot(hy_ref[h_plane], w_in_ref[:, cols])

        def out_piece(p):
            cols = slice(p * PIECE_COLS, (p + 1) * PIECE_COLS)
            out_ref[:, cols] = xc_ref[:, cols] + _dot(hy_ref[y_plane[prev]], w_out_ref[:, cols])

        hy_ref[h_plane] = _rms(xa_ref[...], mix_norm_ref[...]).astype(BF16)
        pieces = ([functools.partial(out_piece, p) for p in range(D_MODEL // PIECE_COLS)]
                  + [functools.partial(in_piece, p) for p in range(IN_COLS // PIECE_COLS)])
        pieces.reverse()
        for b in range(n_blocks):
            if pieces:
                pieces.pop()()
            for _ in block(b, proj_refs[prev], y_refs[cur]):
                if pieces:
                    pieces.pop()()
        while pieces:
            pieces.pop()()

    for parity in range(2):
        pl.when((step % 2) == parity)(functools.partial(body, parity, 1 - parity))

    kbuf_ref[0:BLK, :] = kbuf_ref[MIX_TILE:MIX_TILE + BLK, :]
    vbuf_ref[0:BLK, :] = vbuf_ref[MIX_TILE:MIX_TILE + BLK, :]


def _mixer(x2d, cos, sin, sinks, mix_norm, w_in, q_norm, k_norm, gv_norm, ws, gbias,
           attn_norm, gm_norm, w_out, tiles_per_batch):
    tokens = x2d.shape[0]
    n_tiles = tokens // MIX_TILE

    def stage(lag, w):
        return pl.BlockSpec((MIX_TILE, w), lambda i: (jnp.clip(i - lag, 0, n_tiles - 1), 0))

    return pl.pallas_call(
        functools.partial(_mixer_kernel, tiles_per_batch),
        out_shape=jax.ShapeDtypeStruct((tokens, D_MODEL), F32),
        grid=(n_tiles + 2,),
        in_specs=[pl.BlockSpec(memory_space=pltpu.SMEM),
                  stage(0, D_MODEL), stage(2, D_MODEL), stage(1, LANES), stage(1, LANES),
                  _const_spec((1, D_MODEL)), _const_spec((D_MODEL, IN_COLS)),
                  _const_spec((1, LANES)), _const_spec((1, LANES)),
                  _const_spec((1, GM_W)), _const_spec((GM_W // HEAD_DIM, BLK, BLK)),
                  _const_spec((BLK, GM_W)),
                  _const_spec((1, ATTN_W)), _const_spec((1, GM_W)),
                  _const_spec((D_MODEL, D_MODEL))],
        out_specs=stage(2, D_MODEL),
        scratch_shapes=[
            pltpu.VMEM((2, MIX_TILE, IN_COLS), F32),
            pltpu.VMEM((3, MIX_TILE, D_MODEL), BF16),
            pltpu.VMEM((MIX_TILE + BLK, KV_W), F32),
            pltpu.VMEM((MIX_TILE + BLK, KV_W), F32),
            pltpu.VMEM((GM_W // LANES, BLK, 2 * BLK), BF16),
            pltpu.VMEM((2, 2 * BLK, 4 * BLK), F32),
        ],
        compiler_params=pltpu.CompilerParams(
            dimension_semantics=("arbitrary",), vmem_limit_bytes=VMEM_LIMIT),
        name="mixer",
    )(sinks, x2d, x2d, cos, sin, mix_norm, w_in, q_norm, k_norm,
      gv_norm, ws, gbias, attn_norm, gm_norm, w_out)


def _xattn_kernel(x_ref, xa_norm_ref, wq_ref, q_norm_ref, k_ref, v_ref, wo_ref,
                  w_up_f32_ref, w_down_f32_ref,
                  out_ref, w_up_ref, w_down_ref):
    w_up_ref[...] = w_up_f32_ref[...].astype(BF16)
    w_down_ref[...] = w_down_f32_ref[...].astype(BF16)

    x = x_ref[...]
    h = _rms(x, xa_norm_ref[...]).astype(BF16)
    q = _dot(h, wq_ref[...])
    outs = []
    for hd in range(XA_HEADS):
        cols = slice(hd * XA_DH, (hd + 1) * XA_DH)
        qh = _rms(q[:, cols], q_norm_ref[...]).astype(BF16)
        s = _dot_nt(qh, k_ref[:, cols])
        p = jnp.exp(s - jnp.max(s, axis=-1, keepdims=True))
        inv = 1.0 / jnp.sum(p, axis=-1, keepdims=True)
        outs.append((_dot(p.astype(BF16), v_ref[:, cols]) * inv).astype(BF16))
    out_ref[...] = x + _dot(jnp.concatenate(outs, axis=1), wo_ref[...])


def _passenger_spec(shape, n_steps):
    rows, cols = shape
    n_blocks = next(n for n in range(n_steps, 0, -1)
                    if n_steps % n == 0 and rows % n == 0 and (rows // n) % (2 * SUBLANES) == 0)
    visits = n_steps // n_blocks
    return pl.BlockSpec((rows // n_blocks, cols), lambda i: (i // visits, 0))


def _xattn(x2d, xa_norm, wq, q_norm, k, v, wo, tiles_per_batch, ffn_up, ffn_down):
    tokens = x2d.shape[0]
    n_tiles = tokens // TILE_M
    row = pl.BlockSpec((TILE_M, D_MODEL), lambda i: (i, 0))
    kv = pl.BlockSpec((MEM_LEN, D_MODEL), lambda i: (i // tiles_per_batch, 0))
    up_spec = _passenger_spec(ffn_up.shape, n_tiles)
    down_spec = _passenger_spec(ffn_down.shape, n_tiles)
    return pl.pallas_call(
        _xattn_kernel,
        out_shape=(jax.ShapeDtypeStruct((tokens, D_MODEL), F32),
                   jax.ShapeDtypeStruct(ffn_up.shape, BF16),
                   jax.ShapeDtypeStruct(ffn_down.shape, BF16)),
        grid=(n_tiles,),
        in_specs=[row, _const_spec((1, D_MODEL)), _const_spec((D_MODEL, D_MODEL)),
                  _const_spec((1, XA_DH)), kv, kv, _const_spec((D_MODEL, D_MODEL)),
                  up_spec, down_spec],
        out_specs=(row, up_spec, down_spec),
        compiler_params=pltpu.CompilerParams(
            dimension_semantics=("arbitrary",), vmem_limit_bytes=VMEM_LIMIT),
        name="xattn",
    )(x2d, xa_norm, wq, q_norm, k, v, wo, ffn_up, ffn_down)


def _ffn_kernel(tiles_per_batch,
                x_ref, norm_ref, w_up_ref, conv_ref, conv_b_ref, w_down_ref,
                out_ref,
                xp_ref, op_ref, h_ref, carry_ref, g_ref):
    step = pl.program_id(0)
    nb = FFN_TILE // SUBLANES

    @pl.when((step % tiles_per_batch) == 0)
    def _reset_carry():
        carry_ref[...] = jnp.zeros(carry_ref.shape, F32)

    n_lane_groups = D_MODEL // LANES
    pitch = nb + SUBLANES
    for i in range(nb):
        for lg in range(n_lane_groups):
            xp_ref[lg, pl.ds(i, SUBLANES, stride=pitch), :] = (
                x_ref[i * SUBLANES:(i + 1) * SUBLANES, lg * LANES:(lg + 1) * LANES])
    xp = jnp.concatenate(
        [jnp.concatenate([xp_ref[lg, r * pitch:r * pitch + nb, :] for r in range(SUBLANES)], axis=0)
         for lg in range(n_lane_groups)], axis=1)
    h_ref[...] = _rms(xp, norm_ref[...]).astype(BF16)

    first_row = lax.broadcasted_iota(jnp.int32, (SUBLANES, FF_CHUNK), 0) == 0

    def shift_stream(blk, carried):
        rolled = pltpu.roll(blk, 1, 0)
        head = jnp.where(first_row, carried, rolled[0:SUBLANES])
        return jnp.concatenate([head, rolled[SUBLANES:]], axis=0), rolled[0:SUBLANES]

    def conv(a, cols):
        s6, c6 = shift_stream(a[6 * nb:7 * nb], carry_ref[0, :, cols])
        s7, c7 = shift_stream(a[7 * nb:8 * nb], carry_ref[1, :, cols])
        carry_ref[0, :, cols] = c6
        carry_ref[1, :, cols] = c7
        back1 = jnp.concatenate([s7, a[0:7 * nb]], axis=0)
        back2 = jnp.concatenate([s6, s7, a[0:6 * nb]], axis=0)
        return (conv_b_ref[:, cols] + back2 * conv_ref[0:1, cols]
                + back1 * conv_ref[1:2, cols] + a * conv_ref[2:3, cols])

    halves = [None, None]
    for j in range(N_FF_CHUNKS):
        for part in range(2):
            c0 = part * D_FF + j * FF_CHUNK
            cols = slice(c0, c0 + FF_CHUNK)
            halves[part] = conv(_dot(h_ref[...], w_up_ref[:, cols]), cols)
        g_ref[:, j * FF_CHUNK:(j + 1) * FF_CHUNK] = (
            _gelu_tanh(halves[0]) * halves[1]).astype(BF16)
    groups_per_piece = PIECE_COLS // LANES
    for p in range(D_MODEL // PIECE_COLS):
        cols = slice(p * PIECE_COLS, (p + 1) * PIECE_COLS)
        y = _dot(g_ref[...], w_down_ref[:, cols])
        for r in range(SUBLANES):
            rows = slice(r * nb, (r + 1) * nb)
            for k in range(groups_per_piece):
                op_ref[p * groups_per_piece + k, pl.ds(r, nb, stride=SUBLANES), :] = (
                    y[rows, k * LANES:(k + 1) * LANES])
        y_nat = jnp.concatenate(
            [op_ref[p * groups_per_piece + k] for k in range(groups_per_piece)], axis=1)
        out_ref[:, cols] = x_ref[:, cols] + y_nat


def _ffn(x2d, norm, w_up, conv, conv_b, w_down, tiles_per_batch):
    tokens = x2d.shape[0]
    row = pl.BlockSpec((FFN_TILE, D_MODEL), lambda i: (i, 0))
    return pl.pallas_call(
        functools.partial(_ffn_kernel, tiles_per_batch),
        out_shape=jax.ShapeDtypeStruct((tokens, D_MODEL), F32),
        grid=(tokens // FFN_TILE,),
        in_specs=[row, _const_spec((1, D_MODEL)),
                  _const_spec((D_MODEL, 2 * D_FF)),
                  _const_spec((CONV_W, 2 * D_FF)),
                  _const_spec((1, 2 * D_FF)),
                  _const_spec((D_FF, D_MODEL))],
        out_specs=row,
        scratch_shapes=[
            pltpu.VMEM((D_MODEL // LANES, FFN_TILE + SUBLANES * SUBLANES, LANES), F32),
            pltpu.VMEM((D_MODEL // LANES, FFN_TILE, LANES), F32),
            pltpu.VMEM((FFN_TILE, D_MODEL), BF16),
            pltpu.VMEM((2, SUBLANES, 2 * D_FF), F32),
            pltpu.VMEM((FFN_TILE, D_FF), BF16),
        ],
        compiler_params=pltpu.CompilerParams(
            dimension_semantics=("arbitrary",), vmem_limit_bytes=VMEM_LIMIT),
        name="conv_ffn",
    )(x2d, norm, w_up, conv, conv_b, w_down)


def kernel(x, mem, positions, mix_norm, w_in, q_norm, k_norm, attn_sinks, gmlp_v_norm, gmlp_ws, gmlp_bs, attn_out_norm, gmlp_out_norm, w_out, xa_norm, mem_norm, xa_wq, xa_wkv, xa_q_norm, xa_k_norm, xa_wo, ffn_norm, ffn_up, ffn_conv, ffn_conv_b, ffn_down):
    batch, seq, d_model = x.shape
    depth = mix_norm.shape[0]
    assert d_model == D_MODEL and mem.shape[1] == MEM_LEN
    assert seq % MIX_TILE == 0 and seq % FFN_TILE == 0 and seq % TILE_M == 0
    tokens = batch * seq

    inv_freq = 1.0 / (ROPE_THETA ** (jnp.arange(HALF, dtype=F32) * (2.0 / HEAD_DIM)))
    proj_weights = [w[l] for l in range(depth) for w in (w_in, w_out, xa_wkv, xa_wq, xa_wo)]
    cos, sin, proj_bf16 = _rope_tables(positions, inv_freq, proj_weights)
    reps = LANES // HEAD_DIM

    x2d = x.reshape(tokens, D_MODEL)
    mem2d = mem.reshape(batch * MEM_LEN, D_MODEL)
    row = lambda a: a.reshape(1, -1)
    for l in range(depth):
        gbias = jnp.repeat(gmlp_bs[l].T, HEAD_DIM, axis=1)
        w_in_l, w_out_l, wkv_l, wq_l, wo_l = proj_bf16[5 * l:5 * l + 5]
        x2d = _mixer(x2d, cos, sin, attn_sinks[l], row(mix_norm[l]), w_in_l,
                     row(jnp.tile(q_norm[l], reps)), row(jnp.tile(k_norm[l], reps)),
                     row(gmlp_v_norm[l]), gmlp_ws[l], gbias,
                     row(attn_out_norm[l]), row(gmlp_out_norm[l]), w_out_l,
                     seq // MIX_TILE)
        k_mem, v_mem = _mem_kv(mem2d, row(mem_norm[l]), wkv_l, row(xa_k_norm[l]), batch)
        x2d, w_up, w_down = _xattn(x2d, row(xa_norm[l]), wq_l, row(xa_q_norm[l]),
                                   k_mem, v_mem, wo_l, seq // TILE_M,
                                   ffn_up[l], ffn_down[l])
        x2d = _ffn(x2d, row(ffn_norm[l]), w_up, ffn_conv[l],
                   row(ffn_conv_b[l]), w_down, seq // FFN_TILE)
    return x2d.reshape(batch, seq, D_MODEL)
```

```python
import functools
import math

import jax
import jax.numpy as jnp
from jax import lax
from jax.experimental import pallas as pl
from jax.experimental.pallas import tpu as pltpu

F32 = jnp.float32
BF16 = jnp.bfloat16

LANES = 128
SUBLANES = 8

D_MODEL = 1024
HEAD_DIM = 64
HALF = HEAD_DIM // 2
ATTN_W = 512
KV_W = 128
GM_W = 512
IN_COLS = ATTN_W + 2 * KV_W + 2 * GM_W
BLK = 128
ROPE_THETA = 10000.0
XA_HEADS = 4
XA_DH = 256
MEM_LEN = 256
D_FF = 2816
FF_CHUNK = 256
N_FF_CHUNKS = D_FF // FF_CHUNK
ROPE_ROWS = 8
PIECE_COLS = 256
GZ0 = ATTN_W + 2 * KV_W
CONV_W = 3
EPS = 1e-6
MASK_BIAS = -1e30

SUB_M = 512
MIX_TILE = SUB_M
FFN_TILE = 2 * SUB_M
TILE_M = 2 * SUB_M
VMEM_LIMIT = 56 * 1024 * 1024


def _rms(x, g):
    return x * lax.rsqrt(jnp.mean(x * x, axis=-1, keepdims=True) + EPS) * g


def _gelu_tanh(x):
    c = math.sqrt(2.0 / math.pi)
    half_x = 0.5 * x
    return half_x + half_x * jnp.tanh(x * (c + (c * 0.044715) * (x * x)))


def _dot(a, b):
    return jnp.dot(a, b, preferred_element_type=F32)


def _dot_nt(a, b):
    return lax.dot_general(a, b, (((1,), (1,)), ((), ())), preferred_element_type=F32)


def _const_spec(shape):
    nd = len(shape)
    return pl.BlockSpec(shape, lambda i: (0,) * nd, pipeline_mode=pl.Buffered(1))


def _rope_kernel(inv_freq_ref, pos_ref, cos_ref, sin_ref):
    inv_freq = inv_freq_ref[...]
    for g in range(ROPE_ROWS):
        pos = pos_ref[g:g + 1, :].astype(F32)
        ang = pos * inv_freq
        c = jnp.cos(ang)
        s = jnp.sin(ang)
        reps = LANES // HEAD_DIM
        c_all = jnp.concatenate([c, c] * reps, axis=0)
        s_all = jnp.concatenate([-s, s] * reps, axis=0)
        rows = slice(g * LANES, (g + 1) * LANES)
        cos_ref[rows, :] = c_all.T
        sin_ref[rows, :] = s_all.T


def _rope_tables(positions, inv_freq):
    tokens = positions.size
    pos2d = positions.reshape(tokens // LANES, LANES)
    out = jax.ShapeDtypeStruct((tokens, LANES), F32)
    out_spec = pl.BlockSpec((ROPE_ROWS * LANES, LANES), lambda i: (i, 0))
    return pl.pallas_call(
        _rope_kernel,
        out_shape=(out, out),
        grid=(tokens // (ROPE_ROWS * LANES),),
        in_specs=[_const_spec((HALF, 1)),
                  pl.BlockSpec((ROPE_ROWS, LANES), lambda i: (i, 0))],
        out_specs=(out_spec, out_spec),
        compiler_params=pltpu.CompilerParams(dimension_semantics=("arbitrary",)),
        name="rope_tables",
    )(inv_freq.reshape(HALF, 1), pos2d)


def _mem_kv_kernel(mem_ref, mem_norm_ref, wkv_ref, k_norm_ref, k_ref, v_ref):
    m = _rms(mem_ref[...], mem_norm_ref[...]).astype(BF16)
    kv = _dot(m, wkv_ref[...])
    scale = 1.0 / math.sqrt(XA_DH)
    for h in range(XA_HEADS):
        cols = slice(h * XA_DH, (h + 1) * XA_DH)
        k_ref[:, cols] = (_rms(kv[:, cols], k_norm_ref[...]) * scale).astype(BF16)
    v_ref[...] = kv[:, D_MODEL:].astype(BF16)


def _mem_kv(mem2d, mem_norm, wkv, k_norm, batch):
    out = jax.ShapeDtypeStruct((batch * MEM_LEN, D_MODEL), BF16)
    blk = pl.BlockSpec((MEM_LEN, D_MODEL), lambda b: (b, 0))
    return pl.pallas_call(
        _mem_kv_kernel,
        out_shape=(out, out),
        grid=(batch,),
        in_specs=[blk, _const_spec((1, D_MODEL)), _const_spec((D_MODEL, 2 * D_MODEL)),
                  _const_spec((1, XA_DH))],
        out_specs=(blk, blk),
        compiler_params=pltpu.CompilerParams(
            dimension_semantics=("arbitrary",), vmem_limit_bytes=VMEM_LIMIT),
        name="mem_kv",
    )(mem2d, mem_norm, wkv, k_norm)


def _mixer_kernel(tiles_per_batch,
                  sinks_ref, xa_ref, xc_ref, cos_ref, sin_ref, mix_norm_ref, w_in_ref,
                  q_norm_ref, k_norm_ref, gv_norm_ref, ws_ref, gbias_ref,
                  attn_norm_ref, gm_norm_ref, w_out_ref,
                  out_ref,
                  proj2_ref, hy_ref,
                  kbuf_ref, vbuf_ref, wcat_ref, bias_ref):
    step = pl.program_id(0)
    first_tile = ((step + tiles_per_batch - 1) % tiles_per_batch) == 0
    n_blocks = MIX_TILE // BLK
    proj_refs = (proj2_ref.at[0], proj2_ref.at[1])
    h_plane, y_plane = 0, (1, 2)
    y_refs = (hy_ref.at[y_plane[0]], hy_ref.at[y_plane[1]])

    @pl.when(step == 0)
    def _build_constants():
        t = lax.broadcasted_iota(jnp.int32, (BLK, BLK), 0)
        s = lax.broadcasted_iota(jnp.int32, (BLK, BLK), 1)
        for j in range(GM_W // LANES):
            w_a = jnp.where(s <= t, ws_ref[2 * j], 0.0)
            w_b = jnp.where(s <= t, ws_ref[2 * j + 1], 0.0)
            wcat_ref[j] = jnp.concatenate([w_a, w_b], axis=1).astype(BF16)
        qi = lax.broadcasted_iota(jnp.int32, (2 * BLK, 4 * BLK), 0) & (BLK - 1)
        kj = lax.broadcasted_iota(jnp.int32, (2 * BLK, 4 * BLK), 1) & (2 * BLK - 1)
        cur = (kj >= BLK) & ((kj - BLK) <= qi)
        prev = (kj < BLK) & (kj > qi)
        bias_ref[0] = jnp.where(cur | prev, 0.0, MASK_BIAS)
        bias_ref[1] = jnp.where(cur, 0.0, MASK_BIAS)
        proj2_ref[1] = jnp.zeros((MIX_TILE, IN_COLS), F32)
        hy_ref[y_plane[1]] = jnp.zeros((MIX_TILE, D_MODEL), BF16)

    @pl.when(first_tile)
    def _reset_carry():
        kbuf_ref[0:BLK, :] = jnp.zeros((BLK, KV_W), F32)
        vbuf_ref[0:BLK, :] = jnp.zeros((BLK, KV_W), F32)

    lane = lax.broadcasted_iota(jnp.int32, (BLK, LANES), 1)
    low_head = lane < HEAD_DIM
    first_half = (lane & (HEAD_DIM - 1)) < HALF
    r = lax.broadcasted_iota(jnp.int32, (LANES, LANES), 0)
    c = lax.broadcasted_iota(jnp.int32, (LANES, LANES), 1)
    head_ones = jnp.where((r < HEAD_DIM) == (c < HEAD_DIM), 1.0, 0.0).astype(BF16)
    lane2 = lax.broadcasted_iota(jnp.int32, (2 * BLK, LANES), 1)
    low_head2 = lane2 < HEAD_DIM

    def head_norm_rope(t, gain, cos, sin):
        ss = _dot((t * t).astype(BF16), head_ones)
        tn = t * lax.rsqrt(ss * (1.0 / HEAD_DIM) + EPS) * gain
        rot = jnp.where(first_half, pltpu.roll(tn, LANES - HALF, 1), pltpu.roll(tn, HALF, 1))
        return tn * cos + rot * sin

    def block(b, proj_ref, y_ref):
        r0 = b * BLK
        rows = pl.ds(r0, BLK)
        cos = cos_ref[rows, :]
        sin = sin_ref[rows, :]

        k = head_norm_rope(proj_ref[rows, ATTN_W:ATTN_W + KV_W], k_norm_ref[...], cos, sin)
        kbuf_ref[pl.ds(r0 + BLK, BLK), :] = k
        vbuf_ref[pl.ds(r0 + BLK, BLK), :] = proj_ref[rows, ATTN_W + KV_W:ATTN_W + 2 * KV_W]
        kcat = kbuf_ref[pl.ds(r0, 2 * BLK), :]
        vcat = vbuf_ref[pl.ds(r0, 2 * BLK), :]
        kswp = pltpu.roll(kcat, HEAD_DIM, 1)
        vswp = pltpu.roll(vcat, HEAD_DIM, 1)

        bias = bias_ref[first_tile.astype(jnp.int32)] if b == 0 else bias_ref[0]

        attn_parts = []
        for g in range(2):
            k_lo, k_hi = (kcat, kswp) if g == 0 else (kswp, kcat)
            v_lo, v_hi = (vcat, vswp) if g == 0 else (vswp, vcat)
            k_big = jnp.concatenate([jnp.where(low_head2, k_lo, 0.0),
                                     jnp.where(low_head2, 0.0, k_hi)], axis=0).astype(BF16)
            v_big = jnp.concatenate([jnp.where(low_head2, v_lo, 0.0),
                                     jnp.where(low_head2, 0.0, v_hi)], axis=0).astype(BF16)
            q_pairs = []
            for pair in (2 * g, 2 * g + 1):
                qcols = slice(pair * LANES, (pair + 1) * LANES)
                q = head_norm_rope(proj_ref[rows, qcols], q_norm_ref[...], cos, sin)
                q_pairs.append((q * (1.0 / math.sqrt(HEAD_DIM))).astype(BF16))
            qs = jnp.concatenate(q_pairs, axis=0)
            s = _dot_nt(qs, k_big) + bias
            yield
            p_rows, inv_rows = [], []
            for rg in range(2):
                p_cols, inv_cols = [], []
                for half in range(2):
                    sink = sinks_ref[4 * g + 2 * rg + half]
                    sh = s[rg * BLK:(rg + 1) * BLK, half * 2 * BLK:(half + 1) * 2 * BLK]
                    m = jnp.maximum(jnp.max(sh, axis=-1, keepdims=True), sink)
                    p = jnp.exp(sh - m)
                    denom = jnp.sum(p, axis=-1, keepdims=True) + jnp.exp(sink - m)
                    p_cols.append(p.astype(BF16))
                    inv_cols.append(1.0 / denom)
                p_rows.append(jnp.concatenate(p_cols, axis=1))
                inv_rows.append(jnp.where(low_head, inv_cols[0], inv_cols[1]))
            o = _dot(jnp.concatenate(p_rows, axis=0), v_big)
            attn_parts.append(o[0:BLK] * inv_rows[0])
            attn_parts.append(o[BLK:2 * BLK] * inv_rows[1])
        attn = jnp.concatenate(attn_parts, axis=1)
        y_ref[rows, 0:ATTN_W] = _rms(attn, attn_norm_ref[...]).astype(BF16)
        yield

        gu = _gelu_tanh(proj_ref[rows, GZ0:GZ0 + GM_W])
        gv = _rms(_gelu_tanh(proj_ref[rows, GZ0 + GM_W:GZ0 + 2 * GM_W]), gv_norm_ref[...])
        mixed = []
        for j in range(GM_W // LANES):
            vb = gv[:, j * LANES:(j + 1) * LANES]
            v_bd = jnp.concatenate([jnp.where(low_head, vb, 0.0),
                                    jnp.where(low_head, 0.0, vb)], axis=0).astype(BF16)
            mixed.append(_dot(wcat_ref[j], v_bd))
        gm = gu * (jnp.concatenate(mixed, axis=1) + gbias_ref[...])
        y_ref[rows, ATTN_W:ATTN_W + GM_W] = _rms(gm, gm_norm_ref[...]).astype(BF16)

    def body(cur, prev):
        def in_piece(p):
            cols = slice(p * PIECE_COLS, (p + 1) * PIECE_COLS)
            proj_refs[cur][:, cols] = _dot(hy_ref[h_plane], w_in_ref[:, cols])

        def out_piece(p):
            cols = slice(p * PIECE_COLS, (p + 1) * PIECE_COLS)
            out_ref[:, cols] = xc_ref[:, cols] + _dot(hy_ref[y_plane[prev]], w_out_ref[:, cols])

        hy_ref[h_plane] = _rms(xa_ref[...], mix_norm_ref[...]).astype(BF16)
        pieces = ([functools.partial(out_piece, p) for p in range(D_MODEL // PIECE_COLS)]
                  + [functools.partial(in_piece, p) for p in range(IN_COLS // PIECE_COLS)])
        pieces.reverse()
        for b in range(n_blocks):
            if pieces:
                pieces.pop()()
            for _ in block(b, proj_refs[prev], y_refs[cur]):
                if pieces:
                    pieces.pop()()
        while pieces:
            pieces.pop()()

    for parity in range(2):
        pl.when((step % 2) == parity)(functools.partial(body, parity, 1 - parity))

    kbuf_ref[0:BLK, :] = kbuf_ref[MIX_TILE:MIX_TILE + BLK, :]
    vbuf_ref[0:BLK, :] = vbuf_ref[MIX_TILE:MIX_TILE + BLK, :]


def _mixer(x2d, cos, sin, sinks, mix_norm, w_in, q_norm, k_norm, gv_norm, ws, gbias,
           attn_norm, gm_norm, w_out, tiles_per_batch):
    tokens = x2d.shape[0]
    n_tiles = tokens // MIX_TILE

    def stage(lag, w):
        return pl.BlockSpec((MIX_TILE, w), lambda i: (jnp.clip(i - lag, 0, n_tiles - 1), 0))

    return pl.pallas_call(
        functools.partial(_mixer_kernel, tiles_per_batch),
        out_shape=jax.ShapeDtypeStruct((tokens, D_MODEL), F32),
        grid=(n_tiles + 2,),
        in_specs=[pl.BlockSpec(memory_space=pltpu.SMEM),
                  stage(0, D_MODEL), stage(2, D_MODEL), stage(1, LANES), stage(1, LANES),
                  _const_spec((1, D_MODEL)), _const_spec((D_MODEL, IN_COLS)),
                  _const_spec((1, LANES)), _const_spec((1, LANES)),
                  _const_spec((1, GM_W)), _const_spec((GM_W // HEAD_DIM, BLK, BLK)),
                  _const_spec((BLK, GM_W)),
                  _const_spec((1, ATTN_W)), _const_spec((1, GM_W)),
                  _const_spec((D_MODEL, D_MODEL))],
        out_specs=stage(2, D_MODEL),
        scratch_shapes=[
            pltpu.VMEM((2, MIX_TILE, IN_COLS), F32),
            pltpu.VMEM((3, MIX_TILE, D_MODEL), BF16),
            pltpu.VMEM((MIX_TILE + BLK, KV_W), F32),
            pltpu.VMEM((MIX_TILE + BLK, KV_W), F32),
            pltpu.VMEM((GM_W // LANES, BLK, 2 * BLK), BF16),
            pltpu.VMEM((2, 2 * BLK, 4 * BLK), F32),
        ],
        compiler_params=pltpu.CompilerParams(
            dimension_semantics=("arbitrary",), vmem_limit_bytes=VMEM_LIMIT),
        name="mixer",
    )(sinks, x2d, x2d, cos, sin, mix_norm, w_in, q_norm, k_norm,
      gv_norm, ws, gbias, attn_norm, gm_norm, w_out)


def _xattn_kernel(x_ref, xa_norm_ref, wq_ref, q_norm_ref, k_ref, v_ref, wo_ref,
                  w_up_f32_ref, w_down_f32_ref,
                  out_ref, w_up_ref, w_down_ref):
    w_up_ref[...] = w_up_f32_ref[...].astype(BF16)
    w_down_ref[...] = w_down_f32_ref[...].astype(BF16)

    x = x_ref[...]
    h = _rms(x, xa_norm_ref[...]).astype(BF16)
    q = _dot(h, wq_ref[...])
    outs = []
    for hd in range(XA_HEADS):
        cols = slice(hd * XA_DH, (hd + 1) * XA_DH)
        halves = []
        for hf in range(TILE_M // SUB_M):
            rows = slice(hf * SUB_M, (hf + 1) * SUB_M)
            qh = _rms(q[rows, cols], q_norm_ref[...]).astype(BF16)
            s = _dot_nt(qh, k_ref[:, cols])
            p = jnp.exp(s - jnp.max(s, axis=-1, keepdims=True))
            inv = 1.0 / jnp.sum(p, axis=-1, keepdims=True)
            halves.append((_dot(p.astype(BF16), v_ref[:, cols]) * inv).astype(BF16))
        outs.append(jnp.concatenate(halves, axis=0))
    out_ref[...] = x + _dot(jnp.concatenate(outs, axis=1), wo_ref[...])


def _passenger_spec(shape, n_steps):
    rows, cols = shape
    n_blocks = next(n for n in range(n_steps, 0, -1)
                    if n_steps % n == 0 and rows % n == 0 and (rows // n) % (2 * SUBLANES) == 0)
    visits = n_steps // n_blocks
    return pl.BlockSpec((rows // n_blocks, cols), lambda i: (i // visits, 0))


def _xattn(x2d, xa_norm, wq, q_norm, k, v, wo, tiles_per_batch, ffn_up, ffn_down):
    tokens = x2d.shape[0]
    n_tiles = tokens // TILE_M
    row = pl.BlockSpec((TILE_M, D_MODEL), lambda i: (i, 0))
    kv = pl.BlockSpec((MEM_LEN, D_MODEL), lambda i: (i // tiles_per_batch, 0))
    up_spec = _passenger_spec(ffn_up.shape, n_tiles)
    down_spec = _passenger_spec(ffn_down.shape, n_tiles)
    return pl.pallas_call(
        _xattn_kernel,
        out_shape=(jax.ShapeDtypeStruct((tokens, D_MODEL), F32),
                   jax.ShapeDtypeStruct(ffn_up.shape, BF16),
                   jax.ShapeDtypeStruct(ffn_down.shape, BF16)),
        grid=(n_tiles,),
        in_specs=[row, _const_spec((1, D_MODEL)), _const_spec((D_MODEL, D_MODEL)),
                  _const_spec((1, XA_DH)), kv, kv, _const_spec((D_MODEL, D_MODEL)),
                  up_spec, down_spec],
        out_specs=(row, up_spec, down_spec),
        compiler_params=pltpu.CompilerParams(
            dimension_semantics=("arbitrary",), vmem_limit_bytes=VMEM_LIMIT),
        name="xattn",
    )(x2d, xa_norm, wq, q_norm, k, v, wo, ffn_up, ffn_down)


def _ffn_kernel(tiles_per_batch,
                x_ref, norm_ref, w_up_ref, conv_ref, conv_b_ref, w_down_ref,
                out_ref,
                xp_ref, op_ref, h_ref, carry_ref, g_ref):
    step = pl.program_id(0)
    nb = FFN_TILE // SUBLANES

    @pl.when((step % tiles_per_batch) == 0)
    def _reset_carry():
        carry_ref[...] = jnp.zeros(carry_ref.shape, F32)

    n_lane_groups = D_MODEL // LANES
    pitch = nb + SUBLANES
    for i in range(nb):
        for lg in range(n_lane_groups):
            xp_ref[lg, pl.ds(i, SUBLANES, stride=pitch), :] = (
                x_ref[i * SUBLANES:(i + 1) * SUBLANES, lg * LANES:(lg + 1) * LANES])
    xp = jnp.concatenate(
        [jnp.concatenate([xp_ref[lg, r * pitch:r * pitch + nb, :] for r in range(SUBLANES)], axis=0)
         for lg in range(n_lane_groups)], axis=1)
    h_ref[...] = _rms(xp, norm_ref[...]).astype(BF16)

    first_row = lax.broadcasted_iota(jnp.int32, (SUBLANES, FF_CHUNK), 0) == 0

    def shift_stream(blk, carried):
        rolled = pltpu.roll(blk, 1, 0)
        head = jnp.where(first_row, carried, rolled[0:SUBLANES])
        return jnp.concatenate([head, rolled[SUBLANES:]], axis=0), rolled[0:SUBLANES]

    def conv(a, cols):
        s6, c6 = shift_stream(a[6 * nb:7 * nb], carry_ref[0, :, cols])
        s7, c7 = shift_stream(a[7 * nb:8 * nb], carry_ref[1, :, cols])
        carry_ref[0, :, cols] = c6
        carry_ref[1, :, cols] = c7
        back1 = jnp.concatenate([s7, a[0:7 * nb]], axis=0)
        back2 = jnp.concatenate([s6, s7, a[0:6 * nb]], axis=0)
        return (conv_b_ref[:, cols] + back2 * conv_ref[0:1, cols]
                + back1 * conv_ref[1:2, cols] + a * conv_ref[2:3, cols])

    halves = [None, None]
    for j in range(N_FF_CHUNKS):
        for part in range(2):
            c0 = part * D_FF + j * FF_CHUNK
            cols = slice(c0, c0 + FF_CHUNK)
            halves[part] = conv(_dot(h_ref[...], w_up_ref[:, cols]), cols)
        g_ref[:, j * FF_CHUNK:(j + 1) * FF_CHUNK] = (
            _gelu_tanh(halves[0]) * halves[1]).astype(BF16)
    groups_per_piece = PIECE_COLS // LANES
    for p in range(D_MODEL // PIECE_COLS):
        cols = slice(p * PIECE_COLS, (p + 1) * PIECE_COLS)
        y = _dot(g_ref[...], w_down_ref[:, cols])
        for r in range(SUBLANES):
            rows = slice(r * nb, (r + 1) * nb)
            for k in range(groups_per_piece):
                op_ref[p * groups_per_piece + k, pl.ds(r, nb, stride=SUBLANES), :] = (
                    y[rows, k * LANES:(k + 1) * LANES])
        y_nat = jnp.concatenate(
            [op_ref[p * groups_per_piece + k] for k in range(groups_per_piece)], axis=1)
        out_ref[:, cols] = x_ref[:, cols] + y_nat


def _ffn(x2d, norm, w_up, conv, conv_b, w_down, tiles_per_batch):
    tokens = x2d.shape[0]
    row = pl.BlockSpec((FFN_TILE, D_MODEL), lambda i: (i, 0))
    return pl.pallas_call(
        functools.partial(_ffn_kernel, tiles_per_batch),
        out_shape=jax.ShapeDtypeStruct((tokens, D_MODEL), F32),
        grid=(tokens // FFN_TILE,),
        in_specs=[row, _const_spec((1, D_MODEL)),
                  _const_spec((D_MODEL, 2 * D_FF)),
                  _const_spec((CONV_W, 2 * D_FF)),
                  _const_spec((1, 2 * D_FF)),
                  _const_spec((D_FF, D_MODEL))],
        out_specs=row,
        scratch_shapes=[
            pltpu.VMEM((D_MODEL // LANES, FFN_TILE + SUBLANES * SUBLANES, LANES), F32),
            pltpu.VMEM((D_MODEL // LANES, FFN_TILE, LANES), F32),
            pltpu.VMEM((FFN_TILE, D_MODEL), BF16),
            pltpu.VMEM((2, SUBLANES, 2 * D_FF), F32),
            pltpu.VMEM((FFN_TILE, D_FF), BF16),
        ],
        compiler_params=pltpu.CompilerParams(
            dimension_semantics=("arbitrary",), vmem_limit_bytes=VMEM_LIMIT),
        name="conv_ffn",
    )(x2d, norm, w_up, conv, conv_b, w_down)


def kernel(x, mem, positions, mix_norm, w_in, q_norm, k_norm, attn_sinks, gmlp_v_norm, gmlp_ws, gmlp_bs, attn_out_norm, gmlp_out_norm, w_out, xa_norm, mem_norm, xa_wq, xa_wkv, xa_q_norm, xa_k_norm, xa_wo, ffn_norm, ffn_up, ffn_conv, ffn_conv_b, ffn_down):
    batch, seq, d_model = x.shape
    depth = mix_norm.shape[0]
    assert d_model == D_MODEL and mem.shape[1] == MEM_LEN
    assert seq % MIX_TILE == 0 and seq % FFN_TILE == 0 and seq % TILE_M == 0
    tokens = batch * seq

    inv_freq = 1.0 / (ROPE_THETA ** (jnp.arange(HALF, dtype=F32) * (2.0 / HEAD_DIM)))
    cos, sin = _rope_tables(positions, inv_freq)
    reps = LANES // HEAD_DIM

    x2d = x.reshape(tokens, D_MODEL)
    mem2d = mem.reshape(batch * MEM_LEN, D_MODEL)
    row = lambda a: a.reshape(1, -1)
    for l in range(depth):
        gbias = jnp.repeat(gmlp_bs[l].T, HEAD_DIM, axis=1)
        x2d = _mixer(x2d, cos, sin, attn_sinks[l], row(mix_norm[l]), w_in[l].astype(BF16),
                     row(jnp.tile(q_norm[l], reps)), row(jnp.tile(k_norm[l], reps)),
                     row(gmlp_v_norm[l]), gmlp_ws[l], gbias,
                     row(attn_out_norm[l]), row(gmlp_out_norm[l]), w_out[l].astype(BF16),
                     seq // MIX_TILE)
        k_mem, v_mem = _mem_kv(mem2d, row(mem_norm[l]), xa_wkv[l].astype(BF16),
                               row(xa_k_norm[l]), batch)
        x2d, w_up, w_down = _xattn(x2d, row(xa_norm[l]), xa_wq[l].astype(BF16), row(xa_q_norm[l]),
                                   k_mem, v_mem, xa_wo[l].astype(BF16), seq // TILE_M,
                                   ffn_up[l], ffn_down[l])
        x2d = _ffn(x2d, row(ffn_norm[l]), w_up, ffn_conv[l],
                   row(ffn_conv_b[l]), w_down, seq // FFN_TILE)
    return x2d.reshape(batch, seq, D_MODEL)
```

```python
import functools
import math

import jax
import jax.numpy as jnp
from jax import lax
from jax.experimental import pallas as pl
from jax.experimental.pallas import tpu as pltpu

F32 = jnp.float32
BF16 = jnp.bfloat16

LANES = 128
SUBLANES = 8

D_MODEL = 1024
HEAD_DIM = 64
HALF = HEAD_DIM // 2
ATTN_W = 512
KV_W = 128
GM_W = 512
IN_COLS = ATTN_W + 2 * KV_W + 2 * GM_W
BLK = 128
ROPE_THETA = 10000.0
XA_HEADS = 4
XA_DH = 256
MEM_LEN = 256
D_FF = 2816
FF_CHUNK = 256
N_FF_CHUNKS = D_FF // FF_CHUNK
ROPE_ROWS = 8
PIECE_COLS = 256
GZ0 = ATTN_W + 2 * KV_W
CONV_W = 3
EPS = 1e-6
MASK_BIAS = -1e30

SUB_M = 512
MIX_TILE = SUB_M
FFN_TILE = 2 * SUB_M
TILE_M = 2 * SUB_M
VMEM_LIMIT = 56 * 1024 * 1024


def _rms(x, g):
    return x * lax.rsqrt(jnp.mean(x * x, axis=-1, keepdims=True) + EPS) * g


def _gelu_tanh(x):
    c = math.sqrt(2.0 / math.pi)
    half_x = 0.5 * x
    return half_x + half_x * jnp.tanh(x * (c + (c * 0.044715) * (x * x)))


def _dot(a, b):
    return jnp.dot(a, b, preferred_element_type=F32)


def _dot_nt(a, b):
    return lax.dot_general(a, b, (((1,), (1,)), ((), ())), preferred_element_type=F32)


def _const_spec(shape):
    nd = len(shape)
    return pl.BlockSpec(shape, lambda i: (0,) * nd, pipeline_mode=pl.Buffered(1))


def _rope_kernel(inv_freq_ref, pos_ref, cos_ref, sin_ref):
    inv_freq = inv_freq_ref[...]
    for g in range(ROPE_ROWS):
        pos = pos_ref[g:g + 1, :].astype(F32)
        ang = pos * inv_freq
        c = jnp.cos(ang)
        s = jnp.sin(ang)
        reps = LANES // HEAD_DIM
        c_all = jnp.concatenate([c, c] * reps, axis=0)
        s_all = jnp.concatenate([-s, s] * reps, axis=0)
        rows = slice(g * LANES, (g + 1) * LANES)
        cos_ref[rows, :] = c_all.T
        sin_ref[rows, :] = s_all.T


def _rope_tables(positions, inv_freq):
    tokens = positions.size
    pos2d = positions.reshape(tokens // LANES, LANES)
    out = jax.ShapeDtypeStruct((tokens, LANES), F32)
    out_spec = pl.BlockSpec((ROPE_ROWS * LANES, LANES), lambda i: (i, 0))
    return pl.pallas_call(
        _rope_kernel,
        out_shape=(out, out),
        grid=(tokens // (ROPE_ROWS * LANES),),
        in_specs=[_const_spec((HALF, 1)),
                  pl.BlockSpec((ROPE_ROWS, LANES), lambda i: (i, 0))],
        out_specs=(out_spec, out_spec),
        compiler_params=pltpu.CompilerParams(dimension_semantics=("arbitrary",)),
        name="rope_tables",
    )(inv_freq.reshape(HALF, 1), pos2d)


def _mem_kv_kernel(mem_ref, mem_norm_ref, wkv_ref, k_norm_ref, k_ref, v_ref):
    m = _rms(mem_ref[...], mem_norm_ref[...]).astype(BF16)
    kv = _dot(m, wkv_ref[...])
    scale = 1.0 / math.sqrt(XA_DH)
    for h in range(XA_HEADS):
        cols = slice(h * XA_DH, (h + 1) * XA_DH)
        k_ref[:, cols] = (_rms(kv[:, cols], k_norm_ref[...]) * scale).astype(BF16)
    v_ref[...] = kv[:, D_MODEL:].astype(BF16)


def _mem_kv(mem2d, mem_norm, wkv, k_norm, batch):
    out = jax.ShapeDtypeStruct((batch * MEM_LEN, D_MODEL), BF16)
    blk = pl.BlockSpec((MEM_LEN, D_MODEL), lambda b: (b, 0))
    return pl.pallas_call(
        _mem_kv_kernel,
        out_shape=(out, out),
        grid=(batch,),
        in_specs=[blk, _const_spec((1, D_MODEL)), _const_spec((D_MODEL, 2 * D_MODEL)),
                  _const_spec((1, XA_DH))],
        out_specs=(blk, blk),
        compiler_params=pltpu.CompilerParams(
            dimension_semantics=("arbitrary",), vmem_limit_bytes=VMEM_LIMIT),
        name="mem_kv",
    )(mem2d, mem_norm, wkv, k_norm)


def _mixer_kernel(tiles_per_batch,
                  sinks_ref, xa_ref, xc_ref, cos_ref, sin_ref, mix_norm_ref, w_in_ref,
                  q_norm_ref, k_norm_ref, gv_norm_ref, ws_ref, gbias_ref,
                  attn_norm_ref, gm_norm_ref, w_out_ref,
                  out_ref,
                  proj2_ref, hy_ref,
                  kbuf_ref, vbuf_ref, wcat_ref, bias_ref):
    step = pl.program_id(0)
    first_tile = ((step + tiles_per_batch - 1) % tiles_per_batch) == 0
    n_blocks = MIX_TILE // BLK
    proj_refs = (proj2_ref.at[0], proj2_ref.at[1])
    h_plane, y_plane = 0, (1, 2)
    y_refs = (hy_ref.at[y_plane[0]], hy_ref.at[y_plane[1]])

    @pl.when(step == 0)
    def _build_constants():
        t = lax.broadcasted_iota(jnp.int32, (BLK, BLK), 0)
        s = lax.broadcasted_iota(jnp.int32, (BLK, BLK), 1)
        for j in range(GM_W // LANES):
            w_a = jnp.where(s <= t, ws_ref[2 * j], 0.0)
            w_b = jnp.where(s <= t, ws_ref[2 * j + 1], 0.0)
            wcat_ref[j] = jnp.concatenate([w_a, w_b], axis=1).astype(BF16)
        qi = lax.broadcasted_iota(jnp.int32, (2 * BLK, 4 * BLK), 0) & (BLK - 1)
        kj = lax.broadcasted_iota(jnp.int32, (2 * BLK, 4 * BLK), 1) & (2 * BLK - 1)
        cur = (kj >= BLK) & ((kj - BLK) <= qi)
        prev = (kj < BLK) & (kj > qi)
        bias_ref[0] = jnp.where(cur | prev, 0.0, MASK_BIAS)
        bias_ref[1] = jnp.where(cur, 0.0, MASK_BIAS)
        proj2_ref[1] = jnp.zeros((MIX_TILE, IN_COLS), F32)
        hy_ref[y_plane[1]] = jnp.zeros((MIX_TILE, D_MODEL), BF16)

    @pl.when(first_tile)
    def _reset_carry():
        kbuf_ref[0:BLK, :] = jnp.zeros((BLK, KV_W), F32)
        vbuf_ref[0:BLK, :] = jnp.zeros((BLK, KV_W), F32)

    lane = lax.broadcasted_iota(jnp.int32, (BLK, LANES), 1)
    low_head = lane < HEAD_DIM
    first_half = (lane & (HEAD_DIM - 1)) < HALF
    r = lax.broadcasted_iota(jnp.int32, (LANES, LANES), 0)
    c = lax.broadcasted_iota(jnp.int32, (LANES, LANES), 1)
    head_ones = jnp.where((r < HEAD_DIM) == (c < HEAD_DIM), 1.0, 0.0).astype(BF16)
    lane2 = lax.broadcasted_iota(jnp.int32, (2 * BLK, LANES), 1)
    low_head2 = lane2 < HEAD_DIM

    def head_norm_rope(t, gain, cos, sin):
        ss = _dot((t * t).astype(BF16), head_ones)
        tn = t * lax.rsqrt(ss * (1.0 / HEAD_DIM) + EPS) * gain
        rot = jnp.where(first_half, pltpu.roll(tn, LANES - HALF, 1), pltpu.roll(tn, HALF, 1))
        return tn * cos + rot * sin

    def block(b, proj_ref, y_ref):
        r0 = b * BLK
        rows = pl.ds(r0, BLK)
        cos = cos_ref[rows, :]
        sin = sin_ref[rows, :]

        k = head_norm_rope(proj_ref[rows, ATTN_W:ATTN_W + KV_W], k_norm_ref[...], cos, sin)
        kbuf_ref[pl.ds(r0 + BLK, BLK), :] = k
        vbuf_ref[pl.ds(r0 + BLK, BLK), :] = proj_ref[rows, ATTN_W + KV_W:ATTN_W + 2 * KV_W]
        kcat = kbuf_ref[pl.ds(r0, 2 * BLK), :]
        vcat = vbuf_ref[pl.ds(r0, 2 * BLK), :]
        kswp = pltpu.roll(kcat, HEAD_DIM, 1)
        vswp = pltpu.roll(vcat, HEAD_DIM, 1)

        bias = bias_ref[first_tile.astype(jnp.int32)] if b == 0 else bias_ref[0]

        attn_parts = []
        for g in range(2):
            k_lo, k_hi = (kcat, kswp) if g == 0 else (kswp, kcat)
            v_lo, v_hi = (vcat, vswp) if g == 0 else (vswp, vcat)
            k_big = jnp.concatenate([jnp.where(low_head2, k_lo, 0.0),
                                     jnp.where(low_head2, 0.0, k_hi)], axis=0).astype(BF16)
            v_big = jnp.concatenate([jnp.where(low_head2, v_lo, 0.0),
                                     jnp.where(low_head2, 0.0, v_hi)], axis=0).astype(BF16)
            q_pairs = []
            for pair in (2 * g, 2 * g + 1):
                qcols = slice(pair * LANES, (pair + 1) * LANES)
                q = head_norm_rope(proj_ref[rows, qcols], q_norm_ref[...], cos, sin)
                q_pairs.append((q * (1.0 / math.sqrt(HEAD_DIM))).astype(BF16))
            qs = jnp.concatenate(q_pairs, axis=0)
            s = _dot_nt(qs, k_big) + bias
            yield
            p_rows, inv_rows = [], []
            for rg in range(2):
                p_cols, inv_cols = [], []
                for half in range(2):
                    sink = sinks_ref[4 * g + 2 * rg + half]
                    sh = s[rg * BLK:(rg + 1) * BLK, half * 2 * BLK:(half + 1) * 2 * BLK]
                    m = jnp.maximum(jnp.max(sh, axis=-1, keepdims=True), sink)
                    p = jnp.exp(sh - m)
                    denom = jnp.sum(p, axis=-1, keepdims=True) + jnp.exp(sink - m)
                    p_cols.append(p.astype(BF16))
                    inv_cols.append(1.0 / denom)
                p_rows.append(jnp.concatenate(p_cols, axis=1))
                inv_rows.append(jnp.where(low_head, inv_cols[0], inv_cols[1]))
            o = _dot(jnp.concatenate(p_rows, axis=0), v_big)
            attn_parts.append(o[0:BLK] * inv_rows[0])
            attn_parts.append(o[BLK:2 * BLK] * inv_rows[1])
        attn = jnp.concatenate(attn_parts, axis=1)
        y_ref[rows, 0:ATTN_W] = _rms(attn, attn_norm_ref[...]).astype(BF16)
        yield

        gu = _gelu_tanh(proj_ref[rows, GZ0:GZ0 + GM_W])
        gv = _rms(_gelu_tanh(proj_ref[rows, GZ0 + GM_W:GZ0 + 2 * GM_W]), gv_norm_ref[...])
        mixed = []
        for j in range(GM_W // LANES):
            vb = gv[:, j * LANES:(j + 1) * LANES]
            v_bd = jnp.concatenate([jnp.where(low_head, vb, 0.0),
                                    jnp.where(low_head, 0.0, vb)], axis=0).astype(BF16)
            mixed.append(_dot(wcat_ref[j], v_bd))
        gm = gu * (jnp.concatenate(mixed, axis=1) + gbias_ref[...])
        y_ref[rows, ATTN_W:ATTN_W + GM_W] = _rms(gm, gm_norm_ref[...]).astype(BF16)

    def body(cur, prev):
        def in_piece(p):
            cols = slice(p * PIECE_COLS, (p + 1) * PIECE_COLS)
            proj_refs[cur][:, cols] = _dot(hy_ref[h_plane], w_in_ref[:, cols])

        def out_piece(p):
            cols = slice(p * PIECE_COLS, (p + 1) * PIECE_COLS)
            out_ref[:, cols] = xc_ref[:, cols] + _dot(hy_ref[y_plane[prev]], w_out_ref[:, cols])

        hy_ref[h_plane] = _rms(xa_ref[...], mix_norm_ref[...]).astype(BF16)
        pieces = ([functools.partial(out_piece, p) for p in range(D_MODEL // PIECE_COLS)]
                  + [functools.partial(in_piece, p) for p in range(IN_COLS // PIECE_COLS)])
        pieces.reverse()
        for b in range(n_blocks):
            if pieces:
                pieces.pop()()
            for _ in block(b, proj_refs[prev], y_refs[cur]):
                if pieces:
                    pieces.pop()()
        while pieces:
            pieces.pop()()

    for parity in range(2):
        pl.when((step % 2) == parity)(functools.partial(body, parity, 1 - parity))

    kbuf_ref[0:BLK, :] = kbuf_ref[MIX_TILE:MIX_TILE + BLK, :]
    vbuf_ref[0:BLK, :] = vbuf_ref[MIX_TILE:MIX_TILE + BLK, :]


def _mixer(x2d, cos, sin, sinks, mix_norm, w_in, q_norm, k_norm, gv_norm, ws, gbias,
           attn_norm, gm_norm, w_out, tiles_per_batch):
    tokens = x2d.shape[0]
    n_tiles = tokens // MIX_TILE

    def stage(lag, w):
        return pl.BlockSpec((MIX_TILE, w), lambda i: (jnp.clip(i - lag, 0, n_tiles - 1), 0))

    return pl.pallas_call(
        functools.partial(_mixer_kernel, tiles_per_batch),
        out_shape=jax.ShapeDtypeStruct((tokens, D_MODEL), F32),
        grid=(n_tiles + 2,),
        in_specs=[pl.BlockSpec(memory_space=pltpu.SMEM),
                  stage(0, D_MODEL), stage(2, D_MODEL), stage(1, LANES), stage(1, LANES),
                  _const_spec((1, D_MODEL)), _const_spec((D_MODEL, IN_COLS)),
                  _const_spec((1, LANES)), _const_spec((1, LANES)),
                  _const_spec((1, GM_W)), _const_spec((GM_W // HEAD_DIM, BLK, BLK)),
                  _const_spec((BLK, GM_W)),
                  _const_spec((1, ATTN_W)), _const_spec((1, GM_W)),
                  _const_spec((D_MODEL, D_MODEL))],
        out_specs=stage(2, D_MODEL),
        scratch_shapes=[
            pltpu.VMEM((2, MIX_TILE, IN_COLS), F32),
            pltpu.VMEM((3, MIX_TILE, D_MODEL), BF16),
            pltpu.VMEM((MIX_TILE + BLK, KV_W), F32),
            pltpu.VMEM((MIX_TILE + BLK, KV_W), F32),
            pltpu.VMEM((GM_W // LANES, BLK, 2 * BLK), BF16),
            pltpu.VMEM((2, 2 * BLK, 4 * BLK), F32),
        ],
        compiler_params=pltpu.CompilerParams(
            dimension_semantics=("arbitrary",), vmem_limit_bytes=VMEM_LIMIT),
        name="mixer",
    )(sinks, x2d, x2d, cos, sin, mix_norm, w_in, q_norm, k_norm,
      gv_norm, ws, gbias, attn_norm, gm_norm, w_out)


def _xattn_kernel(x_ref, xa_norm_ref, wq_ref, q_norm_ref, k_ref, v_ref, wo_ref,
                  w_up_f32_ref, w_down_f32_ref,
                  out_ref, w_up_ref, w_down_ref):
    w_up_ref[...] = w_up_f32_ref[...].astype(BF16)
    w_down_ref[...] = w_down_f32_ref[...].astype(BF16)

    x = x_ref[...]
    h = _rms(x, xa_norm_ref[...]).astype(BF16)
    q = _dot(h, wq_ref[...])
    outs = []
    for hd in range(XA_HEADS):
        cols = slice(hd * XA_DH, (hd + 1) * XA_DH)
        halves = []
        for hf in range(TILE_M // SUB_M):
            rows = slice(hf * SUB_M, (hf + 1) * SUB_M)
            qh = _rms(q[rows, cols], q_norm_ref[...]).astype(BF16)
            s = _dot_nt(qh, k_ref[:, cols])
            p = jnp.exp(s - jnp.max(s, axis=-1, keepdims=True))
            inv = 1.0 / jnp.sum(p, axis=-1, keepdims=True)
            halves.append((_dot(p.astype(BF16), v_ref[:, cols]) * inv).astype(BF16))
        outs.append(jnp.concatenate(halves, axis=0))
    out_ref[...] = x + _dot(jnp.concatenate(outs, axis=1), wo_ref[...])


def _passenger_spec(shape, n_steps):
    rows, cols = shape
    n_blocks = next(n for n in range(n_steps, 0, -1)
                    if n_steps % n == 0 and rows % n == 0 and (rows // n) % (2 * SUBLANES) == 0)
    visits = n_steps // n_blocks
    return pl.BlockSpec((rows // n_blocks, cols), lambda i: (i // visits, 0))


def _xattn(x2d, xa_norm, wq, q_norm, k, v, wo, tiles_per_batch, ffn_up, ffn_down):
    tokens = x2d.shape[0]
    n_tiles = tokens // TILE_M
    row = pl.BlockSpec((TILE_M, D_MODEL), lambda i: (i, 0))
    kv = pl.BlockSpec((MEM_LEN, D_MODEL), lambda i: (i // tiles_per_batch, 0))
    up_spec = _passenger_spec(ffn_up.shape, n_tiles)
    down_spec = _passenger_spec(ffn_down.shape, n_tiles)
    return pl.pallas_call(
        _xattn_kernel,
        out_shape=(jax.ShapeDtypeStruct((tokens, D_MODEL), F32),
                   jax.ShapeDtypeStruct(ffn_up.shape, BF16),
                   jax.ShapeDtypeStruct(ffn_down.shape, BF16)),
        grid=(n_tiles,),
        in_specs=[row, _const_spec((1, D_MODEL)), _const_spec((D_MODEL, D_MODEL)),
                  _const_spec((1, XA_DH)), kv, kv, _const_spec((D_MODEL, D_MODEL)),
                  up_spec, down_spec],
        out_specs=(row, up_spec, down_spec),
        compiler_params=pltpu.CompilerParams(
            dimension_semantics=("arbitrary",), vmem_limit_bytes=VMEM_LIMIT),
        name="xattn",
    )(x2d, xa_norm, wq, q_norm, k, v, wo, ffn_up, ffn_down)


def _ffn_kernel(tiles_per_batch,
                x_ref, norm_ref, w_up_ref, conv_ref, conv_b_ref, w_down_ref,
                out_ref,
                xp_ref, op_ref, h_ref, carry_ref, g_ref):
    step = pl.program_id(0)
    nb = FFN_TILE // SUBLANES

    @pl.when((step % tiles_per_batch) == 0)
    def _reset_carry():
        carry_ref[...] = jnp.zeros(carry_ref.shape, F32)

    n_lane_groups = D_MODEL // LANES
    pitch = nb + SUBLANES
    for i in range(nb):
        for lg in range(n_lane_groups):
            xp_ref[lg, pl.ds(i, SUBLANES, stride=pitch), :] = (
                x_ref[i * SUBLANES:(i + 1) * SUBLANES, lg * LANES:(lg + 1) * LANES])
    xp = jnp.concatenate(
        [jnp.concatenate([xp_ref[lg, r * pitch:r * pitch + nb, :] for r in range(SUBLANES)], axis=0)
         for lg in range(n_lane_groups)], axis=1)
    h_ref[...] = _rms(xp, norm_ref[...]).astype(BF16)

    first_row = lax.broadcasted_iota(jnp.int32, (SUBLANES, FF_CHUNK), 0) == 0

    def shift_stream(blk, carried):
        rolled = pltpu.roll(blk, 1, 0)
        head = jnp.where(first_row, carried, rolled[0:SUBLANES])
        return jnp.concatenate([head, rolled[SUBLANES:]], axis=0), rolled[0:SUBLANES]

    def conv(a, cols):
        s6, c6 = shift_stream(a[6 * nb:7 * nb], carry_ref[0, :, cols])
        s7, c7 = shift_stream(a[7 * nb:8 * nb], carry_ref[1, :, cols])
        carry_ref[0, :, cols] = c6
        carry_ref[1, :, cols] = c7
        back1 = jnp.concatenate([s7, a[0:7 * nb]], axis=0)
        back2 = jnp.concatenate([s6, s7, a[0:6 * nb]], axis=0)
        return (conv_b_ref[:, cols] + back2 * conv_ref[0:1, cols]
                + back1 * conv_ref[1:2, cols] + a * conv_ref[2:3, cols])

    halves = [None, None]
    for j in range(N_FF_CHUNKS):
        for part in range(2):
            c0 = part * D_FF + j * FF_CHUNK
            cols = slice(c0, c0 + FF_CHUNK)
            halves[part] = conv(_dot(h_ref[...], w_up_ref[:, cols]), cols)
        for hf in range(FFN_TILE // SUB_M):
            rows = slice(hf * SUB_M, (hf + 1) * SUB_M)
            g_ref[rows, j * FF_CHUNK:(j + 1) * FF_CHUNK] = (
                _gelu_tanh(halves[0][rows]) * halves[1][rows]).astype(BF16)
    groups_per_piece = PIECE_COLS // LANES
    for p in range(D_MODEL // PIECE_COLS):
        cols = slice(p * PIECE_COLS, (p + 1) * PIECE_COLS)
        y = _dot(g_ref[...], w_down_ref[:, cols])
        for r in range(SUBLANES):
            rows = slice(r * nb, (r + 1) * nb)
            for k in range(groups_per_piece):
                op_ref[p * groups_per_piece + k, pl.ds(r, nb, stride=SUBLANES), :] = (
                    y[rows, k * LANES:(k + 1) * LANES])
        y_nat = jnp.concatenate(
            [op_ref[p * groups_per_piece + k] for k in range(groups_per_piece)], axis=1)
        out_ref[:, cols] = x_ref[:, cols] + y_nat


def _ffn(x2d, norm, w_up, conv, conv_b, w_down, tiles_per_batch):
    tokens = x2d.shape[0]
    row = pl.BlockSpec((FFN_TILE, D_MODEL), lambda i: (i, 0))
    return pl.pallas_call(
        functools.partial(_ffn_kernel, tiles_per_batch),
        out_shape=jax.ShapeDtypeStruct((tokens, D_MODEL), F32),
        grid=(tokens // FFN_TILE,),
        in_specs=[row, _const_spec((1, D_MODEL)),
                  _const_spec((D_MODEL, 2 * D_FF)),
                  _const_spec((CONV_W, 2 * D_FF)),
                  _const_spec((1, 2 * D_FF)),
                  _const_spec((D_FF, D_MODEL))],
        out_specs=row,
        scratch_shapes=[
            pltpu.VMEM((D_MODEL // LANES, FFN_TILE + SUBLANES * SUBLANES, LANES), F32),
            pltpu.VMEM((D_MODEL // LANES, FFN_TILE, LANES), F32),
            pltpu.VMEM((FFN_TILE, D_MODEL), BF16),
            pltpu.VMEM((2, SUBLANES, 2 * D_FF), F32),
            pltpu.VMEM((FFN_TILE, D_FF), BF16),
        ],
        compiler_params=pltpu.CompilerParams(
            dimension_semantics=("arbitrary",), vmem_limit_bytes=VMEM_LIMIT),
        name="conv_ffn",
    )(x2d, norm, w_up, conv, conv_b, w_down)


def kernel(x, mem, positions, mix_norm, w_in, q_norm, k_norm, attn_sinks, gmlp_v_norm, gmlp_ws, gmlp_bs, attn_out_norm, gmlp_out_norm, w_out, xa_norm, mem_norm, xa_wq, xa_wkv, xa_q_norm, xa_k_norm, xa_wo, ffn_norm, ffn_up, ffn_conv, ffn_conv_b, ffn_down):
    batch, seq, d_model = x.shape
    depth = mix_norm.shape[0]
    assert d_model == D_MODEL and mem.shape[1] == MEM_LEN
    assert seq % MIX_TILE == 0 and seq % FFN_TILE == 0 and seq % TILE_M == 0
    tokens = batch * seq

    inv_freq = 1.0 / (ROPE_THETA ** (jnp.arange(HALF, dtype=F32) * (2.0 / HEAD_DIM)))
    cos, sin = _rope_tables(positions, inv_freq)
    reps = LANES // HEAD_DIM

    x2d = x.reshape(tokens, D_MODEL)
    mem2d = mem.reshape(batch * MEM_LEN, D_MODEL)
    row = lambda a: a.reshape(1, -1)
    for l in range(depth):
        gbias = jnp.repeat(gmlp_bs[l].T, HEAD_DIM, axis=1)
        x2d = _mixer(x2d, cos, sin, attn_sinks[l], row(mix_norm[l]), w_in[l].astype(BF16),
                     row(jnp.tile(q_norm[l], reps)), row(jnp.tile(k_norm[l], reps)),
                     row(gmlp_v_norm[l]), gmlp_ws[l], gbias,
                     row(attn_out_norm[l]), row(gmlp_out_norm[l]), w_out[l].astype(BF16),
                     seq // MIX_TILE)
        k_mem, v_mem = _mem_kv(mem2d, row(mem_norm[l]), xa_wkv[l].astype(BF16),
                               row(xa_k_norm[l]), batch)
        x2d, w_up, w_down = _xattn(x2d, row(xa_norm[l]), xa_wq[l].astype(BF16), row(xa_q_norm[l]),
                                   k_mem, v_mem, xa_wo[l].astype(BF16), seq // TILE_M,
                                   ffn_up[l], ffn_down[l])
        x2d = _ffn(x2d, row(ffn_norm[l]), w_up, ffn_conv[l],
                   row(ffn_conv_b[l]), w_down, seq // FFN_TILE)
    return x2d.reshape(batch, seq, D_MODEL)
```
